```python
import math
import jax, jax.numpy as jnp
from jax import lax
import numpy as np

D_MODEL = 1024
BATCH = 16
SEQ = 2048
DEPTH = 1

MEM_LEN = 256
HEAD_DIM = 128
ATTN_HEADS = 6
ATTN_KV_HEADS = 2
ATTN_GROUP = ATTN_HEADS // ATTN_KV_HEADS
WINDOW = 128
BLOCK = 128
ROPE_DIM = HEAD_DIM // 4
ROPE_THETA = 500000.0
LRU_WIDTH = 768
LRU_BLOCKS = 6
LRU_BLOCK_W = LRU_WIDTH // LRU_BLOCKS
CONV_WIDTH = 4
CONV_LEFT = CONV_WIDTH // 2
LRU_C = 8.0
MEM_HEADS = 4
N_BRANCH = 3
PEER_HEADS = 8
PEER_KEYS = 128
PEER_EXPERTS = PEER_KEYS * PEER_KEYS
PEER_QDIM = 256
PEER_HALF = PEER_QDIM // 2
PEER_TOPK = 16
PEER_CHUNK = 128
EPS = 1e-6
NEG_INF = -1e30

ATTN_Q_W = ATTN_HEADS * HEAD_DIM
ATTN_KV_W = ATTN_KV_HEADS * HEAD_DIM
MEM_Q_W = MEM_HEADS * HEAD_DIM
GATE_W = N_BRANCH * D_MODEL
IN_W = ATTN_Q_W + 2 * ATTN_KV_W + 2 * LRU_WIDTH + MEM_Q_W + GATE_W

kernel_name = "hybrid_swa_rglru_mem_peer_encoder"


def rmsnorm(x, g):
    xf = x.astype(jnp.float32)
    y = xf * lax.rsqrt(jnp.mean(xf * xf, axis=-1, keepdims=True) + EPS) * g.astype(jnp.float32)
    return y.astype(x.dtype)


def partial_rope(t, positions):
    half = ROPE_DIM // 2
    inv_freq = ROPE_THETA ** (-jnp.arange(0, ROPE_DIM, 2, dtype=jnp.float32) / ROPE_DIM)
    ang = positions.astype(jnp.float32)[..., None] * inv_freq
    cos = jnp.cos(ang)[:, :, None, :]
    sin = jnp.sin(ang)[:, :, None, :]
    tr = t[..., :ROPE_DIM].astype(jnp.float32)
    t1, t2 = tr[..., :half], tr[..., half:]
    rot = jnp.concatenate([t1 * cos - t2 * sin, t2 * cos + t1 * sin], axis=-1)
    return jnp.concatenate([rot.astype(t.dtype), t[..., ROPE_DIM:]], axis=-1)


def band_blocks(t, nb):
    b = t.shape[0]
    tp = jnp.pad(t, ((0, 0), (BLOCK, BLOCK), (0, 0), (0, 0)))
    tp = tp.reshape(b, nb + 2, BLOCK, t.shape[2], t.shape[3])
    return jnp.concatenate([tp[:, :-2], tp[:, 1:-1], tp[:, 2:]], axis=2)


def windowed_gqa(q, k, v, sink):
    b, s = q.shape[0], q.shape[1]
    nb = s // BLOCK
    qb = q.reshape(b, nb, BLOCK, ATTN_KV_HEADS, ATTN_GROUP, HEAD_DIM)
    kb = band_blocks(k, nb)
    vb = band_blocks(v, nb)
    scores = jnp.einsum('bnqkgd,bnskd->bnkgqs', qb, kb).astype(jnp.float32) * (HEAD_DIM ** -0.5)
    blk = jnp.arange(nb, dtype=jnp.int32)[:, None]
    q_pos = blk * BLOCK + jnp.arange(BLOCK, dtype=jnp.int32)[None, :]
    k_pos = blk * BLOCK - BLOCK + jnp.arange(3 * BLOCK, dtype=jnp.int32)[None, :]
    valid = ((k_pos[:, None, :] >= 0) & (k_pos[:, None, :] < s)
             & (jnp.abs(k_pos[:, None, :] - q_pos[:, :, None]) <= WINDOW))
    scores = jnp.where(valid[None, :, None, None], scores, NEG_INF)
    sink_b = jnp.broadcast_to(sink.astype(jnp.float32).reshape(ATTN_KV_HEADS, ATTN_GROUP)[None, None, :, :, None, None],
                              scores.shape[:-1] + (1,))
    p = jax.nn.softmax(jnp.concatenate([scores, sink_b], axis=-1), axis=-1)[..., :-1]
    out = jnp.einsum('bnkgqs,bnskd->bnqkgd', p.astype(v.dtype), vb)
    return out.reshape(b, s, ATTN_Q_W)


def centred_depthwise_conv(t, w, bias):
    s = t.shape[1]
    tp = jnp.pad(t, ((0, 0), (CONV_LEFT, CONV_WIDTH - 1 - CONV_LEFT), (0, 0)))
    y = bias[None, None, :]
    for j in range(CONV_WIDTH):
        y = y + w[j][None, None, :] * tp[:, j:j + s]
    return y


def _lin_comb(c1, c2):
    a1, b1 = c1
    a2, b2 = c2
    return a1 * a2, a2 * b1 + b2


def bidir_rglru(xc, w_r, b_r, w_i, b_i, lam):
    b, s = xc.shape[0], xc.shape[1]
    xf = xc.astype(jnp.float32)
    xblk = xf.reshape(b, s, LRU_BLOCKS, LRU_BLOCK_W)
    r = jax.nn.sigmoid(jnp.einsum('bshc,dhce->dbshe', xblk, w_r.astype(jnp.float32)).reshape(2, b, s, LRU_WIDTH)
                       + b_r.astype(jnp.float32)[:, None, None, :])
    i = jax.nn.sigmoid(jnp.einsum('bshc,dhce->dbshe', xblk, w_i.astype(jnp.float32)).reshape(2, b, s, LRU_WIDTH)
                       + b_i.astype(jnp.float32)[:, None, None, :])
    log_a = -LRU_C * r * jax.nn.softplus(-lam.astype(jnp.float32))[:, None, None, :]
    a = jnp.exp(log_a)
    bx = jnp.sqrt(-jnp.expm1(2.0 * log_a)) * (i * xf[None])
    _, h_fwd = lax.associative_scan(_lin_comb, (a[0], bx[0]), axis=1)
    _, h_bwd = lax.associative_scan(_lin_comb, (a[1], bx[1]), axis=1, reverse=True)
    return (h_fwd + h_bwd).astype(xc.dtype)


def memory_cross_attention(q, k, v):
    scores = jnp.einsum('bshd,bmhd->bhsm', q, k).astype(jnp.float32) * (HEAD_DIM ** -0.5)
    p = jax.nn.softmax(scores, axis=-1)
    out = jnp.einsum('bhsm,bmhd->bshd', p.astype(v.dtype), v)
    return out.reshape(q.shape[0], q.shape[1], MEM_Q_W)


def peer_ffn(h, w_q, sub_keys, u_tab, v_tab):
    b, s, d = h.shape
    q = (h @ w_q).reshape(b, s, PEER_HEADS, 2, PEER_HALF)
    sc = jnp.einsum('bshpc,hpkc->bshpk', q, sub_keys).astype(jnp.float32)
    top_v, top_i = lax.top_k(sc, PEER_TOPK)
    cand_v = (top_v[..., 0, :, None] + top_v[..., 1, None, :]).reshape(b, s, PEER_HEADS, PEER_TOPK * PEER_TOPK)
    cand_i = (top_i[..., 0, :, None] * PEER_KEYS + top_i[..., 1, None, :]).reshape(b, s, PEER_HEADS, PEER_TOPK * PEER_TOPK)
    best_v, best_pos = lax.top_k(cand_v, PEER_TOPK)
    idx = jnp.take_along_axis(cand_i, best_pos, axis=-1)
    gate = jax.nn.softmax(best_v, axis=-1)
    n_tok = b * s
    n_chunk = n_tok // PEER_CHUNK
    hc = h.reshape(n_chunk, PEER_CHUNK, d)
    ic = idx.reshape(n_chunk, PEER_CHUNK, PEER_HEADS, PEER_TOPK)
    gc = gate.reshape(n_chunk, PEER_CHUNK, PEER_HEADS, PEER_TOPK)

    def expert_chunk(args):
        hx, ix, gx = args
        u = u_tab[ix]
        act = jax.nn.gelu(jnp.einsum('chkd,cd->chk', u, hx).astype(jnp.float32))
        vv = v_tab[ix]
        return jnp.einsum('chk,chkd->cd', (gx * act).astype(vv.dtype), vv)

    out = lax.map(expert_chunk, (hc, ic, gc))
    return out.reshape(b, s, d)


def setup_inputs(seed: int = 0) -> dict:
    key = jax.random.key(seed)
    ks = jax.random.split(key, 32)
    f32 = jnp.float32
    nrm = lambda k, shape, scale: jax.random.normal(k, shape, f32) * scale
    u = jax.random.uniform(ks[12], (DEPTH, 2, LRU_WIDTH), f32, 0.9, 0.999)
    a0 = u ** (1.0 / LRU_C)
    lam = jnp.log(a0) - jnp.log1p(-a0)
    return {
        "x": nrm(ks[0], (BATCH, SEQ, D_MODEL), 1.0),
        "mem": nrm(ks[1], (BATCH, MEM_LEN, D_MODEL), 1.0),
        "positions": jnp.broadcast_to(jnp.arange(SEQ, dtype=jnp.int32)[None, :], (BATCH, SEQ)),
        "g_mix": 1.0 + nrm(ks[2], (DEPTH, D_MODEL), 0.02),
        "g_mem": 1.0 + nrm(ks[3], (DEPTH, D_MODEL), 0.02),
        "w_in": nrm(ks[4], (DEPTH, D_MODEL, IN_W), D_MODEL ** -0.5),
        "gate_b": nrm(ks[5], (DEPTH, GATE_W), 0.02),
        "attn_sink": nrm(ks[6], (DEPTH, ATTN_HEADS), 0.5),
        "conv_w": nrm(ks[7], (DEPTH, CONV_WIDTH, LRU_WIDTH), CONV_WIDTH ** -0.5),
        "conv_b": nrm(ks[8], (DEPTH, LRU_WIDTH), 0.02),
        "lru_wr": nrm(ks[9], (DEPTH, 2, LRU_BLOCKS, LRU_BLOCK_W, LRU_BLOCK_W), LRU_BLOCK_W ** -0.5),
        "lru_br": nrm(ks[10], (DEPTH, 2, LRU_WIDTH), 0.02),
        "lru_wi": nrm(ks[11], (DEPTH, 2, LRU_BLOCKS, LRU_BLOCK_W, LRU_BLOCK_W), LRU_BLOCK_W ** -0.5),
        "lru_bi": nrm(ks[13], (DEPTH, 2, LRU_WIDTH), 0.02),
        "lru_lambda": lam,
        "w_mem_kv": nrm(ks[14], (DEPTH, D_MODEL, 2 * MEM_Q_W), D_MODEL ** -0.5),
        "p_attn": nrm(ks[15], (DEPTH, ATTN_Q_W, D_MODEL), ATTN_Q_W ** -0.5),
        "p_lru": nrm(ks[16], (DEPTH, LRU_WIDTH, D_MODEL), LRU_WIDTH ** -0.5),
        "p_mem": nrm(ks[17], (DEPTH, MEM_Q_W, D_MODEL), MEM_Q_W ** -0.5),
        "w_out": nrm(ks[18], (DEPTH, D_MODEL, D_MODEL), D_MODEL ** -0.5),
        "g_ffn": 1.0 + nrm(ks[19], (DEPTH, D_MODEL), 0.02),
        "w_peer_q": nrm(ks[20], (DEPTH, D_MODEL, PEER_HEADS * PEER_QDIM), D_MODEL ** -0.5),
        "peer_sub_keys": nrm(ks[21], (DEPTH, PEER_HEADS, 2, PEER_KEYS, PEER_HALF), PEER_HALF ** -0.5),
        "peer_u": nrm(ks[22], (DEPTH, PEER_EXPERTS, D_MODEL), D_MODEL ** -0.5),
        "peer_v": nrm(ks[23], (DEPTH, PEER_EXPERTS, D_MODEL), 0.25),
        "g_final": 1.0 + nrm(ks[24], (D_MODEL,), 0.02),
    }


def reference(x, mem, positions, g_mix, g_mem, w_in, gate_b, attn_sink, conv_w, conv_b,
              lru_wr, lru_br, lru_wi, lru_bi, lru_lambda, w_mem_kv, p_attn, p_lru, p_mem,
              w_out, g_ffn, w_peer_q, peer_sub_keys, peer_u, peer_v, g_final):
    b, s, d = x.shape
    offs = [int(o) for o in np.cumsum([ATTN_Q_W, ATTN_KV_W, ATTN_KV_W, LRU_WIDTH, LRU_WIDTH, MEM_Q_W])]
    for l in range(DEPTH):
        h = rmsnorm(x, g_mix[l])
        proj = h @ w_in[l]
        q_a, k_a, v_a, x_l, gate_l, q_m, gates = jnp.split(proj, offs, axis=-1)
        q_a = partial_rope(q_a.reshape(b, s, ATTN_HEADS, HEAD_DIM), positions)
        k_a = partial_rope(k_a.reshape(b, s, ATTN_KV_HEADS, HEAD_DIM), positions)
        v_a = v_a.reshape(b, s, ATTN_KV_HEADS, HEAD_DIM)
        attn_out = windowed_gqa(q_a, k_a, v_a, attn_sink[l])
        xc = centred_depthwise_conv(x_l, conv_w[l], conv_b[l])
        lru_out = bidir_rglru(xc, lru_wr[l], lru_br[l], lru_wi[l], lru_bi[l], lru_lambda[l]) * jax.nn.gelu(gate_l)
        kv_m = rmsnorm(mem, g_mem[l]) @ w_mem_kv[l]
        k_m = kv_m[..., :MEM_Q_W].reshape(b, MEM_LEN, MEM_HEADS, HEAD_DIM)
        v_m = kv_m[..., MEM_Q_W:].reshape(b, MEM_LEN, MEM_HEADS, HEAD_DIM)
        mem_out = memory_cross_attention(q_m.reshape(b, s, MEM_HEADS, HEAD_DIM), k_m, v_m)
        g = jax.nn.sigmoid(gates + gate_b[l]).reshape(b, s, N_BRANCH, d)
        merged = (g[:, :, 0] * (attn_out @ p_attn[l])
                  + g[:, :, 1] * (lru_out @ p_lru[l])
                  + g[:, :, 2] * (mem_out @ p_mem[l]))
        x = x + merged @ w_out[l]
        h2 = rmsnorm(x, g_ffn[l])
        x = x + peer_ffn(h2, w_peer_q[l], peer_sub_keys[l], peer_u[l], peer_v[l])
    return rmsnorm(x, g_final)
```

```python
import functools
import math

import numpy as np
import jax
import jax.numpy as jnp
from jax import lax
from jax.experimental import pallas as pl
from jax.experimental.pallas import tpu as pltpu

F32 = jnp.float32
BF16 = jnp.bfloat16

HEAD_DIM = 128
ATTN_HEADS = 6
ATTN_KV_HEADS = 2
ATTN_GROUP = ATTN_HEADS // ATTN_KV_HEADS
WINDOW = 128
ROPE_DIM = HEAD_DIM // 4
ROPE_HALF = ROPE_DIM // 2
ROPE_THETA = 500000.0
LRU_WIDTH = 768
LRU_BLOCKS = 6
LRU_BLOCK_W = LRU_WIDTH // LRU_BLOCKS
CONV_WIDTH = 4
CONV_LEFT = CONV_WIDTH // 2
LRU_C = 8.0
MEM_HEADS = 4
N_BRANCH = 3
PEER_HEADS = 8
PEER_KEYS = 128
PEER_HALF = 128
PEER_TOPK = 16
EPS = 1e-6
NEG_INF = -1e30

ATTN_Q_W = ATTN_HEADS * HEAD_DIM
ATTN_KV_W = ATTN_KV_HEADS * HEAD_DIM
MEM_Q_W = MEM_HEADS * HEAD_DIM
ATTN_SCALE = HEAD_DIM ** -0.5

VMEM_LIMIT_BYTES = 56 * 1024 * 1024

PAIR_ROWS_P = PEER_TOPK // 2


def _params(n_axes=1):
    return pltpu.CompilerParams(
        dimension_semantics=("arbitrary",) * n_axes, vmem_limit_bytes=VMEM_LIMIT_BYTES)


def _rms(x, g):
    return x * lax.rsqrt(jnp.mean(x * x, axis=-1, keepdims=True) + EPS) * g


def _inproj_kernel(x_ref, pos_ref, g_ref, invf_ref, wqkv_ref, wlru_ref, wqm_ref, wg_ref, gb_ref,
                   q_ref, k_ref, v_ref, xl_ref, gg_ref, qm_ref, gs_ref):
    hb = _rms(x_ref[...], g_ref[...]).astype(BF16)
    qkv = jnp.dot(hb, wqkv_ref[...], preferred_element_type=F32)
    ang = pos_ref[...].astype(F32) * invf_ref[...]
    cos = jnp.cos(ang)
    sin = jnp.sin(ang)
    lane = lax.broadcasted_iota(jnp.int32, (1, HEAD_DIM), 1)
    sin_lo = jnp.where(lane < ROPE_HALF, -sin, 0.0)
    sin_hi = jnp.where(lane >= ROPE_HALF, sin, 0.0)

    def rope(t):
        return (t * cos + pltpu.roll(t, HEAD_DIM - ROPE_HALF, 1) * sin_lo
                + pltpu.roll(t, ROPE_HALF, 1) * sin_hi)

    for j in range(ATTN_HEADS):
        sl = slice(j * HEAD_DIM, (j + 1) * HEAD_DIM)
        q_ref[:, sl] = rope(qkv[:, sl]).astype(BF16)
    for j in range(ATTN_KV_HEADS):
        sl = slice(j * HEAD_DIM, (j + 1) * HEAD_DIM)
        k_ref[:, sl] = rope(qkv[:, ATTN_Q_W + j * HEAD_DIM:ATTN_Q_W + (j + 1) * HEAD_DIM]).astype(BF16)
    v_ref[...] = qkv[:, ATTN_Q_W + ATTN_KV_W:].astype(BF16)

    lru = jnp.dot(hb, wlru_ref[...], preferred_element_type=F32)
    xl_ref[...] = lru[:, :LRU_WIDTH]
    gg_ref[...] = jax.nn.gelu(lru[:, LRU_WIDTH:]).astype(BF16)
    qm_ref[...] = jnp.dot(hb, wqm_ref[...], preferred_element_type=F32).astype(BF16)
    gates = jnp.dot(hb, wg_ref[...], preferred_element_type=F32) + gb_ref[...]
    gs_ref[...] = jax.nn.sigmoid(gates).astype(BF16)


def _inproj(x2d, pos2d, g_mix, invf, wqkv, wlru, wqm, wg, gate_b, tm):
    n, d = x2d.shape
    row = lambda w: pl.BlockSpec((tm, w), lambda i: (i, 0))
    full = lambda a: pl.BlockSpec(a.shape, lambda i: (0,) * a.ndim)
    widths = (ATTN_Q_W, ATTN_KV_W, ATTN_KV_W, LRU_WIDTH, LRU_WIDTH, MEM_Q_W, N_BRANCH * d)
    dtypes = (BF16, BF16, BF16, F32, BF16, BF16, BF16)
    return pl.pallas_call(
        _inproj_kernel,
        grid=(n // tm,),
        in_specs=[row(d), row(1), full(g_mix), full(invf), full(wqkv), full(wlru), full(wqm),
                  full(wg), full(gate_b)],
        out_specs=[row(w) for w in widths],
        out_shape=[jax.ShapeDtypeStruct((n, w), t) for w, t in zip(widths, dtypes)],
        compiler_params=_params(),
        name="inproj",
    )(x2d, pos2d, g_mix, invf, wqkv, wlru, wqm, wg, gate_b)


def _attn_kernel(sink_ref, q_ref, k_ref, v_ref, o_ref, *, seq):
    band = 3 * WINDOW
    n_blk = seq // WINDOW

    def block(n, carry):
        qs = pl.multiple_of(n * WINDOW, WINDOW)
        ks = pl.multiple_of(jnp.clip(qs - WINDOW, 0, seq - band), WINDOW)
        q_pos = qs + lax.broadcasted_iota(jnp.int32, (WINDOW, 1), 0)
        k_pos = ks + lax.broadcasted_iota(jnp.int32, (1, band), 1)
        dist = k_pos - q_pos
        valid = jnp.where(dist <= WINDOW, dist, -WINDOW - 1) >= -WINDOW
        for kv in range(ATTN_KV_HEADS):
            kvs = slice(kv * HEAD_DIM, (kv + 1) * HEAD_DIM)
            kb = k_ref[pl.ds(ks, band), kvs]
            vb = v_ref[pl.ds(ks, band), kvs]
            for g in range(ATTN_GROUP):
                h = kv * ATTN_GROUP + g
                hs = slice(h * HEAD_DIM, (h + 1) * HEAD_DIM)
                s = lax.dot_general(q_ref[pl.ds(qs, WINDOW), hs], kb, (((1,), (1,)), ((), ())),
                                    preferred_element_type=F32) * ATTN_SCALE
                s = jnp.where(valid, s, NEG_INF)
                sink = sink_ref[h]
                m = jnp.maximum(jnp.max(s, axis=-1, keepdims=True), sink)
                p = jnp.exp(s - m)
                denom = jnp.sum(p, axis=-1, keepdims=True) + jnp.exp(sink - m)
                o = jnp.dot(p.astype(BF16), vb, preferred_element_type=F32) / denom
                o_ref[pl.ds(qs, WINDOW), hs] = o.astype(BF16)
        return carry

    lax.fori_loop(0, n_blk, block, 0)


def _attn(sink, q, k, v, batch, seq):
    spec = lambda w: pl.BlockSpec((seq, w), lambda b: (b, 0))
    return pl.pallas_call(
        functools.partial(_attn_kernel, seq=seq),
        grid=(batch,),
        in_specs=[pl.BlockSpec(memory_space=pltpu.SMEM), spec(ATTN_Q_W), spec(ATTN_KV_W),
                  spec(ATTN_KV_W)],
        out_specs=spec(ATTN_Q_W),
        out_shape=jax.ShapeDtypeStruct((batch * seq, ATTN_Q_W), BF16),
        compiler_params=_params(),
        name="attn",
    )(sink, q, k, v)


def _scan(a, b, t_idx, seq, reverse):
    s = 1
    while s < seq:
        if reverse:
            ok = t_idx < seq - s
            a_n = pltpu.roll(a, seq - s, 0)
            b_n = pltpu.roll(b, seq - s, 0)
        else:
            ok = t_idx >= s
            a_n = pltpu.roll(a, s, 0)
            b_n = pltpu.roll(b, s, 0)
        b = jnp.where(ok, a * b_n + b, b)
        a = jnp.where(ok, a * a_n, a)
        s *= 2
    return b


def _lru_kernel(xl_ref, gg_ref, cw_ref, cb_ref, wr_ref, br_ref, wi_ref, bi_ref, lam_ref, o_ref,
                *, seq):
    t_idx = lax.broadcasted_iota(jnp.int32, (seq, LRU_BLOCK_W), 0)
    for c in range(LRU_BLOCKS):
        cs = slice(c * LRU_BLOCK_W, (c + 1) * LRU_BLOCK_W)
        x = xl_ref[:, cs]
        xc = cb_ref[:, cs] + cw_ref[2:3, cs] * x
        xc = xc + cw_ref[0:1, cs] * jnp.where(t_idx >= 2, pltpu.roll(x, 2, 0), 0.0)
        xc = xc + cw_ref[1:2, cs] * jnp.where(t_idx >= 1, pltpu.roll(x, 1, 0), 0.0)
        xc = xc + cw_ref[3:4, cs] * jnp.where(t_idx < seq - 1, pltpu.roll(x, seq - 1, 0), 0.0)
        xcb = xc.astype(BF16)
        h = None
        for d in range(2):
            r = jax.nn.sigmoid(jnp.dot(xcb, wr_ref[d, c], preferred_element_type=F32)
                               + br_ref[d:d + 1, cs])
            i = jax.nn.sigmoid(jnp.dot(xcb, wi_ref[d, c], preferred_element_type=F32)
                               + bi_ref[d:d + 1, cs])
            z = -lam_ref[d:d + 1, cs]
            softplus = jnp.maximum(z, 0.0) + jnp.log1p(jnp.exp(-jnp.abs(z)))
            log_a = (-LRU_C) * r * softplus
            a = jnp.exp(log_a)
            bx = jnp.sqrt(-jnp.tanh(log_a) * (a * a + 1.0)) * (i * xc)
            hd = _scan(a, bx, t_idx, seq, reverse=(d == 1))
            h = hd if h is None else h + hd
        o_ref[:, cs] = (h * gg_ref[:, cs].astype(F32)).astype(BF16)


def _lru(xl, gg, conv_w, conv_b, wr, br, wi, bi, lam, batch, seq):
    spec = pl.BlockSpec((seq, LRU_WIDTH), lambda b: (b, 0))
    full = lambda a: pl.BlockSpec(a.shape, lambda b: (0,) * a.ndim)
    return pl.pallas_call(
        functools.partial(_lru_kernel, seq=seq),
        grid=(batch,),
        in_specs=[spec, spec, full(conv_w), full(conv_b), full(wr), full(br), full(wi), full(bi),
                  full(lam)],
        out_specs=spec,
        out_shape=jax.ShapeDtypeStruct((batch * seq, LRU_WIDTH), BF16),
        compiler_params=_params(),
        name="lru",
    )(xl, gg, conv_w, conv_b, wr, br, wi, bi, lam)


def _memkv_kernel(mem_ref, g_ref, w_ref, k_ref, v_ref):
    hb = _rms(mem_ref[...], g_ref[...]).astype(BF16)
    kv = jnp.dot(hb, w_ref[...], preferred_element_type=F32)
    k_ref[...] = kv[:, :MEM_Q_W].astype(BF16)
    v_ref[...] = kv[:, MEM_Q_W:].astype(BF16)


def _memkv(mem2d, g_mem, w, batch, mem_len):
    d = mem2d.shape[1]
    full = lambda a: pl.BlockSpec(a.shape, lambda b: (0,) * a.ndim)
    out = pl.BlockSpec((mem_len, MEM_Q_W), lambda b: (b, 0))
    return pl.pallas_call(
        _memkv_kernel,
        grid=(batch,),
        in_specs=[pl.BlockSpec((mem_len, d), lambda b: (b, 0)), full(g_mem), full(w)],
        out_specs=[out, out],
        out_shape=[jax.ShapeDtypeStruct((batch * mem_len, MEM_Q_W), BF16)] * 2,
        compiler_params=_params(),
        name="memkv",
    )(mem2d, g_mem, w)


def _merge_kernel(x_ref, at_ref, lr_ref, qm_ref, gs_ref, km_ref, vm_ref, pa_ref, pl_ref, pm_ref,
                  wo_ref, gf_ref, x1_ref, h2_ref, *, d):
    mem_heads = []
    for h in range(MEM_HEADS):
        hs = slice(h * HEAD_DIM, (h + 1) * HEAD_DIM)
        s = lax.dot_general(qm_ref[:, hs], km_ref[:, hs], (((1,), (1,)), ((), ())),
                            preferred_element_type=F32) * ATTN_SCALE
        p = jnp.exp(s - jnp.max(s, axis=-1, keepdims=True))
        o = jnp.dot(p.astype(BF16), vm_ref[:, hs], preferred_element_type=F32)
        mem_heads.append((o / jnp.sum(p, axis=-1, keepdims=True)).astype(BF16))
    mem_out = jnp.concatenate(mem_heads, axis=-1)
    merged = gs_ref[:, 0:d].astype(F32) * jnp.dot(at_ref[...], pa_ref[...],
                                                  preferred_element_type=F32)
    merged += gs_ref[:, d:2 * d].astype(F32) * jnp.dot(lr_ref[...], pl_ref[...],
                                                       preferred_element_type=F32)
    merged += gs_ref[:, 2 * d:3 * d].astype(F32) * jnp.dot(mem_out, pm_ref[...],
                                                           preferred_element_type=F32)
    x1 = x_ref[...] + jnp.dot(merged.astype(BF16), wo_ref[...], preferred_element_type=F32)
    x1_ref[...] = x1
    h2_ref[...] = _rms(x1, gf_ref[...]).astype(BF16)


def _merge(x2d, attn, lru, qm, gs, km, vm, pa, pl_w, pm, wo, g_ffn, tm, seq, mem_len):
    n, d = x2d.shape
    per_seq = seq // tm
    row = lambda w: pl.BlockSpec((tm, w), lambda i: (i, 0))
    full = lambda a: pl.BlockSpec(a.shape, lambda i: (0,) * a.ndim)
    memspec = pl.BlockSpec((mem_len, MEM_Q_W), lambda i: (i // per_seq, 0))
    return pl.pallas_call(
        functools.partial(_merge_kernel, d=d),
        grid=(n // tm,),
        in_specs=[row(d), row(ATTN_Q_W), row(LRU_WIDTH), row(MEM_Q_W), row(N_BRANCH * d), memspec,
                  memspec, full(pa), full(pl_w), full(pm), full(wo), full(g_ffn)],
        out_specs=[row(d), row(d)],
        out_shape=[jax.ShapeDtypeStruct((n, d), F32), jax.ShapeDtypeStruct((n, d), BF16)],
        compiler_params=_params(),
        name="merge",
    )(x2d, attn, lru, qm, gs, km, vm, pa, pl_w, pm, wo, g_ffn)


def _top16(s):
    k_rows, t = s.shape
    row = lax.broadcasted_iota(jnp.int32, (k_rows, t), 0).astype(F32)
    row16 = lax.broadcasted_iota(jnp.int32, (PEER_TOPK, t), 0)
    work = s
    rank = jnp.full((k_rows, t), float(PEER_TOPK), F32)
    vals = jnp.zeros((PEER_TOPK, t), F32)
    for k in range(PEER_TOPK):
        m = jnp.max(work, axis=0, keepdims=True)
        first = jnp.min(jnp.where(work == m, row, float(k_rows)), axis=0, keepdims=True)
        hit = row == first
        rank = jnp.where(hit, float(k), rank)
        work = jnp.where(hit, -jnp.inf, work)
        vals = jnp.where(row16 == k, m, vals)
    return vals, rank


def _route_kernel(h2_ref, wq_ref, key_ref, cnt_ref, e1_ref, rk2_ref, e2_ref):
    t = h2_ref.shape[0]
    q = jnp.dot(h2_ref[...], wq_ref[...], preferred_element_type=F32).astype(BF16)
    nt = (((1,), (1,)), ((), ()))
    s1 = lax.dot_general(key_ref[0, 0], q[:, :PEER_HALF], nt, preferred_element_type=F32)
    s2 = lax.dot_general(key_ref[0, 1], q[:, PEER_HALF:], nt, preferred_element_type=F32)
    a, rank1 = _top16(s1)
    b, rank2 = _top16(s2)

    rows_a = PAIR_ROWS_P * PEER_TOPK
    q_idx = lax.broadcasted_iota(jnp.int32, (PEER_TOPK, t), 0)
    pieces = [jnp.where(q_idx < PEER_TOPK // (p + 1), a[p:p + 1] + b, -jnp.inf)
              for p in range(PAIR_ROWS_P)]
    pieces.append(a[PAIR_ROWS_P:] + b[0:1])
    cand = jnp.concatenate(pieces, axis=0)
    n_cand = rows_a + PEER_TOPK - PAIR_ROWS_P
    crow = lax.broadcasted_iota(jnp.int32, (n_cand, t), 0).astype(F32)
    work = cand
    sel = jnp.zeros((n_cand, t), F32)
    for _ in range(PEER_TOPK):
        m = jnp.max(work, axis=0, keepdims=True)
        first = jnp.min(jnp.where(work == m, crow, float(n_cand)), axis=0, keepdims=True)
        hit = crow == first
        sel = jnp.where(hit, 1.0, sel)
        work = jnp.where(hit, -jnp.inf, work)
    v0 = a[0:1] + b[0:1]
    z = jnp.sum(sel * jnp.exp(cand - v0), axis=0, keepdims=True)

    cnt = jnp.zeros((PEER_KEYS, t), F32)
    for p in range(PEER_TOPK):
        if p < PAIR_ROWS_P:
            c_p = jnp.sum(sel[p * PEER_TOPK:(p + 1) * PEER_TOPK], axis=0, keepdims=True)
        else:
            c_p = sel[rows_a + p - PAIR_ROWS_P:rows_a + p - PAIR_ROWS_P + 1]
        cnt = jnp.where(rank1 == float(p), c_p, cnt)
    cnt_ref[0] = cnt.astype(BF16)
    e1_ref[0] = jnp.where(rank1 < PEER_TOPK, jnp.exp(s1 - a[0:1]), 0.0).astype(BF16)
    rk2_ref[0] = rank2.astype(BF16)
    e2_ref[0] = jnp.where(rank2 < PEER_TOPK, jnp.exp(s2 - b[0:1]) / z, 0.0).astype(BF16)


def _route(h2, wq, keys, tf):
    n, d = h2.shape
    out = pl.BlockSpec((1, PEER_KEYS, tf), lambda i, h: (h, 0, i))
    return pl.pallas_call(
        _route_kernel,
        grid=(n // tf, PEER_HEADS),
        in_specs=[pl.BlockSpec((tf, d), lambda i, h: (i, 0)),
                  pl.BlockSpec((d, 2 * PEER_HALF), lambda i, h: (0, h)),
                  pl.BlockSpec((1, 2, PEER_KEYS, PEER_HALF), lambda i, h: (h, 0, 0, 0))],
        out_specs=[out] * 4,
        out_shape=[jax.ShapeDtypeStruct((PEER_HEADS, PEER_KEYS, n), BF16)] * 4,
        compiler_params=_params(2),
        name="route",
    )(h2, wq, keys)


def _peer_kernel(h2_ref, x1_ref, u_ref, vt_ref, cnt_ref, e1_ref, rk2_ref, e2_ref, gfin_ref,
                 o_ref, acc_ref, a_ref, w_ref, *, n_i, normalise):
    j = pl.program_id(1)

    @pl.when(j == 0)
    def _():
        acc_ref[...] = jnp.zeros_like(acc_ref)

    a_ref[...] = lax.dot_general(u_ref[...], h2_ref[...], (((1,), (1,)), ((), ())),
                                 preferred_element_type=F32)
    for ii in range(n_i):
        rows = slice(ii * PEER_KEYS, (ii + 1) * PEER_KEYS)
        act = jax.nn.gelu(a_ref[rows, :])
        g = None
        for h in range(PEER_HEADS):
            cnt = cnt_ref[h, ii:ii + 1, :].astype(F32)
            e1 = e1_ref[h, ii:ii + 1, :].astype(F32)
            term = jnp.where(rk2_ref[h].astype(F32) < cnt, e2_ref[h].astype(F32), 0.0) * e1
            g = term if g is None else g + term
        w_ref[rows, :] = (g * act).astype(BF16)
    acc_ref[...] += jnp.dot(vt_ref[...], w_ref[...], preferred_element_type=F32)

    @pl.when(j == pl.num_programs(1) - 1)
    def _():
        x2 = x1_ref[...] + acc_ref[...].T
        o_ref[...] = _rms(x2, gfin_ref[...]) if normalise else x2


def _peer(h2, x1, u, vt, cnt, e1, rk2, e2, g_final, tp, ne, normalise):
    n, d = h2.shape
    n_exp = u.shape[0]
    n_i = ne // PEER_KEYS
    tok = pl.BlockSpec((tp, d), lambda i, j: (i, 0))
    by_i = pl.BlockSpec((PEER_HEADS, n_i, tp), lambda i, j: (0, j, i))
    by_j = pl.BlockSpec((PEER_HEADS, PEER_KEYS, tp), lambda i, j: (0, 0, i))
    return pl.pallas_call(
        functools.partial(_peer_kernel, n_i=n_i, normalise=normalise),
        grid=(n // tp, n_exp // ne),
        in_specs=[tok, tok,
                  pl.BlockSpec((ne, d), lambda i, j: (j, 0)),
                  pl.BlockSpec((d, ne), lambda i, j: (0, j)),
                  by_i, by_i, by_j, by_j,
                  pl.BlockSpec(g_final.shape, lambda i, j: (0, 0))],
        out_specs=tok,
        out_shape=jax.ShapeDtypeStruct((n, d), F32),
        scratch_shapes=[pltpu.VMEM((d, tp), F32), pltpu.VMEM((ne, tp), F32),
                        pltpu.VMEM((ne, tp), BF16)],
        compiler_params=_params(2),
        name="peer",
    )(h2, x1, u, vt, cnt, e1, rk2, e2, g_final)


def _tiles(batch, seq):
    n = batch * seq
    tm = min(256, seq)
    tf = min(256, n)
    tp = min(512, n)
    return tm, tf, tp


def kernel(x, mem, positions, g_mix, g_mem, w_in, gate_b, attn_sink, conv_w, conv_b, lru_wr, lru_br,
           lru_wi, lru_bi, lru_lambda, w_mem_kv, p_attn, p_lru, p_mem, w_out, g_ffn, w_peer_q,
           peer_sub_keys, peer_u, peer_v, g_final):
    batch, seq, d = x.shape
    mem_len = mem.shape[1]
    depth = w_in.shape[0]
    n = batch * seq
    n_exp = peer_u.shape[1]
    tm, tf, tp = _tiles(batch, seq)
    ne = min(2048, n_exp)

    inv_freq = ROPE_THETA ** (-jnp.arange(0, ROPE_DIM, 2, dtype=F32) / ROPE_DIM)
    invf = jnp.zeros((1, HEAD_DIM), F32).at[0, :ROPE_DIM].set(jnp.tile(inv_freq, 2))
    pos2d = positions.reshape(n, 1)
    offs = [int(o) for o in np.cumsum([0, ATTN_Q_W + 2 * ATTN_KV_W, 2 * LRU_WIDTH, MEM_Q_W])]
    mem2d = mem.reshape(batch * mem_len, d)

    xc = x.reshape(n, d)
    out = None
    for l in range(depth):
        w = w_in[l].astype(BF16)
        wqkv, wlru, wqm, wg = (w[:, offs[0]:offs[1]], w[:, offs[1]:offs[2]], w[:, offs[2]:offs[3]],
                               w[:, offs[3]:])
        q, k, v, xl, gg, qm, gs = _inproj(xc, pos2d, g_mix[l][None], invf, wqkv, wlru, wqm, wg,
                                          gate_b[l][None], tm)
        attn = _attn(attn_sink[l], q, k, v, batch, seq)
        lru = _lru(xl, gg, conv_w[l], conv_b[l][None], lru_wr[l].astype(BF16), lru_br[l],
                   lru_wi[l].astype(BF16), lru_bi[l], lru_lambda[l], batch, seq)
        km, vm = _memkv(mem2d, g_mem[l][None], w_mem_kv[l].astype(BF16), batch, mem_len)
        x1, h2 = _merge(xc, attn, lru, qm, gs, km, vm, p_attn[l].astype(BF16),
                        p_lru[l].astype(BF16), p_mem[l].astype(BF16), w_out[l].astype(BF16),
                        g_ffn[l][None], tm, seq, mem_len)
        cnt, e1, rk2, e2 = _route(h2, w_peer_q[l].astype(BF16), peer_sub_keys[l].astype(BF16), tf)
        last = l == depth - 1
        out = _peer(h2, x1, peer_u[l].astype(BF16), peer_v[l].astype(BF16).T, cnt, e1, rk2, e2,
                    g_final[None], tp, ne, normalise=last)
        xc = out
    return out.reshape(batch, seq, d)
```

```python
import functools
import math

import numpy as np
import jax
import jax.numpy as jnp
from jax import lax
from jax.experimental import pallas as pl
from jax.experimental.pallas import tpu as pltpu

F32 = jnp.float32
BF16 = jnp.bfloat16

HEAD_DIM = 128
ATTN_HEADS = 6
ATTN_KV_HEADS = 2
ATTN_GROUP = ATTN_HEADS // ATTN_KV_HEADS
WINDOW = 128
ROPE_DIM = HEAD_DIM // 4
ROPE_HALF = ROPE_DIM // 2
ROPE_THETA = 500000.0
LRU_WIDTH = 768
LRU_BLOCKS = 6
LRU_BLOCK_W = LRU_WIDTH // LRU_BLOCKS
CONV_WIDTH = 4
CONV_LEFT = CONV_WIDTH // 2
LRU_C = 8.0
MEM_HEADS = 4
N_BRANCH = 3
PEER_HEADS = 8
PEER_KEYS = 128
PEER_HALF = 128
PEER_TOPK = 16
EPS = 1e-6
NEG_INF = -1e30

ATTN_Q_W = ATTN_HEADS * HEAD_DIM
ATTN_KV_W = ATTN_KV_HEADS * HEAD_DIM
MEM_Q_W = MEM_HEADS * HEAD_DIM
ATTN_SCALE = HEAD_DIM ** -0.5

VMEM_LIMIT_BYTES = 56 * 1024 * 1024

PAIR_ROWS_P = PEER_TOPK // 2

BF16_ROWS = 16
MXU_DIM = 256
MM_ROWS = 512


def _params(n_axes=1):
    return pltpu.CompilerParams(
        dimension_semantics=("arbitrary",) * n_axes, vmem_limit_bytes=VMEM_LIMIT_BYTES)


def _rms(x, g):
    return x * lax.rsqrt(jnp.mean(x * x, axis=-1, keepdims=True) + EPS) * g


GELU_C0 = 2.0 * math.sqrt(2.0 / math.pi)
GELU_C1 = GELU_C0 * 0.044715


def _gelu_tanh(x):
    return x * jax.nn.sigmoid(x * (GELU_C0 + GELU_C1 * (x * x)))


def _inproj_kernel(x_ref, pos_ref, g_ref, invf_ref, wqkv_ref, wlru_ref, wqm_ref, wg_ref, gb_ref,
                   q_ref, k_ref, v_ref, xl_ref, gg_ref, qm_ref, gs_ref):
    hb = _rms(x_ref[...], g_ref[...]).astype(BF16)
    qkv = jnp.dot(hb, wqkv_ref[...], preferred_element_type=F32)
    ang = pos_ref[...].astype(F32) * invf_ref[...]
    cos = jnp.cos(ang)
    sin = jnp.sin(ang)
    lane = lax.broadcasted_iota(jnp.int32, (1, HEAD_DIM), 1)
    sin_lo = jnp.where(lane < ROPE_HALF, -sin, 0.0)
    sin_hi = jnp.where(lane >= ROPE_HALF, sin, 0.0)

    def rope(t):
        return (t * cos + pltpu.roll(t, HEAD_DIM - ROPE_HALF, 1) * sin_lo
                + pltpu.roll(t, ROPE_HALF, 1) * sin_hi)

    for j in range(ATTN_HEADS):
        sl = slice(j * HEAD_DIM, (j + 1) * HEAD_DIM)
        q_ref[:, sl] = rope(qkv[:, sl]).astype(BF16)
    for j in range(ATTN_KV_HEADS):
        sl = slice(j * HEAD_DIM, (j + 1) * HEAD_DIM)
        k_ref[:, sl] = rope(qkv[:, ATTN_Q_W + j * HEAD_DIM:ATTN_Q_W + (j + 1) * HEAD_DIM]).astype(BF16)
    v_ref[...] = qkv[:, ATTN_Q_W + ATTN_KV_W:].astype(BF16)

    lru = jnp.dot(hb, wlru_ref[...], preferred_element_type=F32)
    xl_ref[...] = lru[:, :LRU_WIDTH]
    gg_ref[...] = jax.nn.gelu(lru[:, LRU_WIDTH:]).astype(BF16)
    qm_ref[...] = jnp.dot(hb, wqm_ref[...], preferred_element_type=F32).astype(BF16)
    gates = jnp.dot(hb, wg_ref[...], preferred_element_type=F32) + gb_ref[...]
    gs_ref[...] = jax.nn.sigmoid(gates).astype(BF16)


def _inproj(x2d, pos2d, g_mix, invf, wqkv, wlru, wqm, wg, gate_b, tm):
    n, d = x2d.shape
    row = lambda w: pl.BlockSpec((tm, w), lambda i: (i, 0))
    full = lambda a: pl.BlockSpec(a.shape, lambda i: (0,) * a.ndim)
    widths = (ATTN_Q_W, ATTN_KV_W, ATTN_KV_W, LRU_WIDTH, LRU_WIDTH, MEM_Q_W, N_BRANCH * d)
    dtypes = (BF16, BF16, BF16, F32, BF16, BF16, BF16)
    return pl.pallas_call(
        _inproj_kernel,
        grid=(n // tm,),
        in_specs=[row(d), row(1), full(g_mix), full(invf), full(wqkv), full(wlru), full(wqm),
                  full(wg), full(gate_b)],
        out_specs=[row(w) for w in widths],
        out_shape=[jax.ShapeDtypeStruct((n, w), t) for w, t in zip(widths, dtypes)],
        compiler_params=_params(),
        name="inproj",
    )(x2d, pos2d, g_mix, invf, wqkv, wlru, wqm, wg, gate_b)


def _attn_kernel(sink_ref, q_ref, k_ref, v_ref, o_ref, *, seq):
    band = 3 * WINDOW
    n_blk = seq // WINDOW

    def block(n, carry):
        qs = pl.multiple_of(n * WINDOW, WINDOW)
        ks = pl.multiple_of(jnp.clip(qs - WINDOW, 0, seq - band), WINDOW)
        q_pos = qs + lax.broadcasted_iota(jnp.int32, (WINDOW, 1), 0)
        k_pos = ks + lax.broadcasted_iota(jnp.int32, (1, band), 1)
        dist = k_pos - q_pos
        valid = jnp.where(dist <= WINDOW, dist, -WINDOW - 1) >= -WINDOW
        for kv in range(ATTN_KV_HEADS):
            kvs = slice(kv * HEAD_DIM, (kv + 1) * HEAD_DIM)
            kb = k_ref[pl.ds(ks, band), kvs]
            vb = v_ref[pl.ds(ks, band), kvs]
            for g in range(ATTN_GROUP):
                h = kv * ATTN_GROUP + g
                hs = slice(h * HEAD_DIM, (h + 1) * HEAD_DIM)
                s = lax.dot_general(q_ref[pl.ds(qs, WINDOW), hs], kb, (((1,), (1,)), ((), ())),
                                    preferred_element_type=F32) * ATTN_SCALE
                s = jnp.where(valid, s, NEG_INF)
                sink = sink_ref[h]
                m = jnp.maximum(jnp.max(s, axis=-1, keepdims=True), sink)
                p = jnp.exp(s - m)
                denom = jnp.sum(p, axis=-1, keepdims=True) + jnp.exp(sink - m)
                o = jnp.dot(p.astype(BF16), vb, preferred_element_type=F32) / denom
                o_ref[pl.ds(qs, WINDOW), hs] = o.astype(BF16)
        return carry

    lax.fori_loop(0, n_blk, block, 0)


def _attn(sink, q, k, v, batch, seq):
    spec = lambda w: pl.BlockSpec((seq, w), lambda b: (b, 0))
    return pl.pallas_call(
        functools.partial(_attn_kernel, seq=seq),
        grid=(batch,),
        in_specs=[pl.BlockSpec(memory_space=pltpu.SMEM), spec(ATTN_Q_W), spec(ATTN_KV_W),
                  spec(ATTN_KV_W)],
        out_specs=spec(ATTN_Q_W),
        out_shape=jax.ShapeDtypeStruct((batch * seq, ATTN_Q_W), BF16),
        compiler_params=_params(),
        name="attn",
    )(sink, q, k, v)


def _scan(a, b, t_idx, seq, reverse):
    s = 1
    while s < seq:
        if reverse:
            ok = t_idx < seq - s
            a_n = pltpu.roll(a, seq - s, 0)
            b_n = pltpu.roll(b, seq - s, 0)
        else:
            ok = t_idx >= s
            a_n = pltpu.roll(a, s, 0)
            b_n = pltpu.roll(b, s, 0)
        b = jnp.where(ok, a * b_n + b, b)
        a = jnp.where(ok, a * a_n, a)
        s *= 2
    return b


def _lru_kernel(xl_ref, gg_ref, cw_ref, cb_ref, wr_ref, br_ref, wi_ref, bi_ref, lam_ref, o_ref,
                *, seq):
    t_idx = lax.broadcasted_iota(jnp.int32, (seq, LRU_BLOCK_W), 0)
    for c in range(LRU_BLOCKS):
        cs = slice(c * LRU_BLOCK_W, (c + 1) * LRU_BLOCK_W)
        x = xl_ref[:, cs]
        xc = cb_ref[:, cs] + cw_ref[2:3, cs] * x
        xc = xc + cw_ref[0:1, cs] * jnp.where(t_idx >= 2, pltpu.roll(x, 2, 0), 0.0)
        xc = xc + cw_ref[1:2, cs] * jnp.where(t_idx >= 1, pltpu.roll(x, 1, 0), 0.0)
        xc = xc + cw_ref[3:4, cs] * jnp.where(t_idx < seq - 1, pltpu.roll(x, seq - 1, 0), 0.0)
        xcb = xc.astype(BF16)
        h = None
        for d in range(2):
            r = jax.nn.sigmoid(jnp.dot(xcb, wr_ref[d, c], preferred_element_type=F32)
                               + br_ref[d:d + 1, cs])
            i = jax.nn.sigmoid(jnp.dot(xcb, wi_ref[d, c], preferred_element_type=F32)
                               + bi_ref[d:d + 1, cs])
            z = -lam_ref[d:d + 1, cs]
            softplus = jnp.maximum(z, 0.0) + jnp.log1p(jnp.exp(-jnp.abs(z)))
            log_a = (-LRU_C) * r * softplus
            a = jnp.exp(log_a)
            bx = jnp.sqrt(-jnp.tanh(log_a) * (a * a + 1.0)) * (i * xc)
            hd = _scan(a, bx, t_idx, seq, reverse=(d == 1))
            h = hd if h is None else h + hd
        o_ref[:, cs] = (h * gg_ref[:, cs].astype(F32)).astype(BF16)


def _lru(xl, gg, conv_w, conv_b, wr, br, wi, bi, lam, batch, seq):
    spec = pl.BlockSpec((seq, LRU_WIDTH), lambda b: (b, 0))
    full = lambda a: pl.BlockSpec(a.shape, lambda b: (0,) * a.ndim)
    return pl.pallas_call(
        functools.partial(_lru_kernel, seq=seq),
        grid=(batch,),
        in_specs=[spec, spec, full(conv_w), full(conv_b), full(wr), full(br), full(wi), full(bi),
                  full(lam)],
        out_specs=spec,
        out_shape=jax.ShapeDtypeStruct((batch * seq, LRU_WIDTH), BF16),
        compiler_params=_params(),
        name="lru",
    )(xl, gg, conv_w, conv_b, wr, br, wi, bi, lam)


def _memkv_kernel(mem_ref, g_ref, w_ref, k_ref, v_ref):
    hb = _rms(mem_ref[...], g_ref[...]).astype(BF16)
    kv = jnp.dot(hb, w_ref[...], preferred_element_type=F32)
    k_ref[...] = kv[:, :MEM_Q_W].astype(BF16)
    v_ref[...] = kv[:, MEM_Q_W:].astype(BF16)


def _memkv(mem2d, g_mem, w, batch, mem_len):
    d = mem2d.shape[1]
    full = lambda a: pl.BlockSpec(a.shape, lambda b: (0,) * a.ndim)
    out = pl.BlockSpec((mem_len, MEM_Q_W), lambda b: (b, 0))
    return pl.pallas_call(
        _memkv_kernel,
        grid=(batch,),
        in_specs=[pl.BlockSpec((mem_len, d), lambda b: (b, 0)), full(g_mem), full(w)],
        out_specs=[out, out],
        out_shape=[jax.ShapeDtypeStruct((batch * mem_len, MEM_Q_W), BF16)] * 2,
        compiler_params=_params(),
        name="memkv",
    )(mem2d, g_mem, w)


def _merge_kernel(x_ref, at_ref, lr_ref, qm_ref, gs_ref, km_ref, vm_ref, pa_ref, pl_ref, pm_ref,
                  wo_ref, gf_ref, x1_ref, h2_ref, *, d):
    mem_heads = []
    for h in range(MEM_HEADS):
        hs = slice(h * HEAD_DIM, (h + 1) * HEAD_DIM)
        s = lax.dot_general(qm_ref[:, hs], km_ref[:, hs], (((1,), (1,)), ((), ())),
                            preferred_element_type=F32) * ATTN_SCALE
        p = jnp.exp(s - jnp.max(s, axis=-1, keepdims=True))
        o = jnp.dot(p.astype(BF16), vm_ref[:, hs], preferred_element_type=F32)
        mem_heads.append((o / jnp.sum(p, axis=-1, keepdims=True)).astype(BF16))
    mem_out = jnp.concatenate(mem_heads, axis=-1)
    merged = gs_ref[:, 0:d].astype(F32) * jnp.dot(at_ref[...], pa_ref[...],
                                                  preferred_element_type=F32)
    merged += gs_ref[:, d:2 * d].astype(F32) * jnp.dot(lr_ref[...], pl_ref[...],
                                                       preferred_element_type=F32)
    merged += gs_ref[:, 2 * d:3 * d].astype(F32) * jnp.dot(mem_out, pm_ref[...],
                                                           preferred_element_type=F32)
    x1 = x_ref[...] + jnp.dot(merged.astype(BF16), wo_ref[...], preferred_element_type=F32)
    x1_ref[...] = x1
    h2_ref[...] = _rms(x1, gf_ref[...]).astype(BF16)


def _merge(x2d, attn, lru, qm, gs, km, vm, pa, pl_w, pm, wo, g_ffn, tm, seq, mem_len):
    n, d = x2d.shape
    per_seq = seq // tm
    row = lambda w: pl.BlockSpec((tm, w), lambda i: (i, 0))
    full = lambda a: pl.BlockSpec(a.shape, lambda i: (0,) * a.ndim)
    memspec = pl.BlockSpec((mem_len, MEM_Q_W), lambda i: (i // per_seq, 0))
    return pl.pallas_call(
        functools.partial(_merge_kernel, d=d),
        grid=(n // tm,),
        in_specs=[row(d), row(ATTN_Q_W), row(LRU_WIDTH), row(MEM_Q_W), row(N_BRANCH * d), memspec,
                  memspec, full(pa), full(pl_w), full(pm), full(wo), full(g_ffn)],
        out_specs=[row(d), row(d)],
        out_shape=[jax.ShapeDtypeStruct((n, d), F32), jax.ShapeDtypeStruct((n, d), BF16)],
        compiler_params=_params(),
        name="merge",
    )(x2d, attn, lru, qm, gs, km, vm, pa, pl_w, pm, wo, g_ffn)


def _top16(s):
    k_rows, t = s.shape
    row = lax.broadcasted_iota(jnp.int32, (k_rows, t), 0).astype(F32)
    row16 = lax.broadcasted_iota(jnp.int32, (PEER_TOPK, t), 0)
    work = s
    rank = jnp.full((k_rows, t), float(PEER_TOPK), F32)
    vals = jnp.zeros((PEER_TOPK, t), F32)
    for k in range(PEER_TOPK):
        m = jnp.max(work, axis=0, keepdims=True)
        first = jnp.min(jnp.where(work == m, row, float(k_rows)), axis=0, keepdims=True)
        hit = row == first
        rank = jnp.where(hit, float(k), rank)
        work = jnp.where(hit, -jnp.inf, work)
        vals = jnp.where(row16 == k, m, vals)
    return vals, rank


def _route_kernel(h2_ref, wq_ref, key_ref, cnt_ref, e1_ref, rk2_ref, e2_ref):
    t = h2_ref.shape[0]
    q = jnp.dot(h2_ref[...], wq_ref[...], preferred_element_type=F32).astype(BF16)
    nt = (((1,), (1,)), ((), ()))
    s1 = lax.dot_general(key_ref[0, 0], q[:, :PEER_HALF], nt, preferred_element_type=F32)
    s2 = lax.dot_general(key_ref[0, 1], q[:, PEER_HALF:], nt, preferred_element_type=F32)
    a, rank1 = _top16(s1)
    b, rank2 = _top16(s2)

    rows_a = PAIR_ROWS_P * PEER_TOPK
    q_idx = lax.broadcasted_iota(jnp.int32, (PEER_TOPK, t), 0)
    pieces = [jnp.where(q_idx < PEER_TOPK // (p + 1), a[p:p + 1] + b, -jnp.inf)
              for p in range(PAIR_ROWS_P)]
    pieces.append(a[PAIR_ROWS_P:] + b[0:1])
    cand = jnp.concatenate(pieces, axis=0)
    n_cand = rows_a + PEER_TOPK - PAIR_ROWS_P
    crow = lax.broadcasted_iota(jnp.int32, (n_cand, t), 0).astype(F32)
    work = cand
    sel = jnp.zeros((n_cand, t), F32)
    for _ in range(PEER_TOPK):
        m = jnp.max(work, axis=0, keepdims=True)
        first = jnp.min(jnp.where(work == m, crow, float(n_cand)), axis=0, keepdims=True)
        hit = crow == first
        sel = jnp.where(hit, 1.0, sel)
        work = jnp.where(hit, -jnp.inf, work)
    v0 = a[0:1] + b[0:1]
    z = jnp.sum(sel * jnp.exp(cand - v0), axis=0, keepdims=True)

    cnt = jnp.zeros((PEER_KEYS, t), F32)
    for p in range(PEER_TOPK):
        if p < PAIR_ROWS_P:
            c_p = jnp.sum(sel[p * PEER_TOPK:(p + 1) * PEER_TOPK], axis=0, keepdims=True)
        else:
            c_p = sel[rows_a + p - PAIR_ROWS_P:rows_a + p - PAIR_ROWS_P + 1]
        cnt = jnp.where(rank1 == float(p), c_p, cnt)
    cnt_ref[0] = cnt
    e1_ref[0] = jnp.where(rank1 < PEER_TOPK, jnp.exp(s1 - a[0:1]), 0.0)
    rk2_ref[0] = rank2.astype(BF16)
    e2_ref[0] = jnp.where(rank2 < PEER_TOPK, jnp.exp(s2 - b[0:1]) / z, 0.0).astype(BF16)


def _route(h2, wq, keys, tf):
    n, d = h2.shape
    by_i = pl.BlockSpec((1, PEER_KEYS, tf), lambda i, h: (h, 0, i))
    by_j = by_i
    by_i_shape = jax.ShapeDtypeStruct((PEER_HEADS, PEER_KEYS, n), F32)
    by_j_shape = jax.ShapeDtypeStruct((PEER_HEADS, PEER_KEYS, n), BF16)
    return pl.pallas_call(
        _route_kernel,
        grid=(n // tf, PEER_HEADS),
        in_specs=[pl.BlockSpec((tf, d), lambda i, h: (i, 0)),
                  pl.BlockSpec((d, 2 * PEER_HALF), lambda i, h: (0, h)),
                  pl.BlockSpec((1, 2, PEER_KEYS, PEER_HALF), lambda i, h: (h, 0, 0, 0))],
        out_specs=[by_i, by_i, by_j, by_j],
        out_shape=[by_i_shape, by_i_shape, by_j_shape, by_j_shape],
        compiler_params=_params(2),
        name="route",
    )(h2, wq, keys)


def _peer_kernel(h2_ref, x1_ref, u_ref, vt_ref, cnt_ref, e1_ref, rk2_ref, e2_ref, gfin_ref,
                 o_ref, acc_ref, a_ref, w_ref, *, n_j, normalise):
    s = pl.program_id(0)
    sb = jnp.maximum(s - 1, 0)
    cur = s % 2
    prev = 1 - cur
    ne, t = a_ref.shape
    d = acc_ref.shape[0]
    n_i = ne // PEER_KEYS

    @pl.when(s == 0)
    def _():
        w_ref[...] = jnp.zeros_like(w_ref)

    @pl.when(sb % n_j == 0)
    def _():
        acc_ref[...] = jnp.zeros_like(acc_ref)

    def mm1(m, n):
        rows = slice(m * MM_ROWS, (m + 1) * MM_ROWS)
        cols = slice(n * MXU_DIM, (n + 1) * MXU_DIM)
        a_ref[rows, cols] = lax.dot_general(u_ref[rows, :], h2_ref[cols, :],
                                            (((1,), (1,)), ((), ())), preferred_element_type=F32)

    def mm2(m, n):
        rows = slice(m * MM_ROWS, (m + 1) * MM_ROWS)
        cols = slice(n * MXU_DIM, (n + 1) * MXU_DIM)
        acc_ref[rows, cols] += jnp.dot(vt_ref[rows, :], w_ref[prev, :, cols],
                                       preferred_element_type=F32)

    def gate_block(ii):
        g = None
        for h in range(PEER_HEADS):
            cnt = jnp.broadcast_to(cnt_ref[h, ii:ii + 1, :], (BF16_ROWS, t)).astype(BF16)
            e1 = jnp.broadcast_to(e1_ref[h, ii:ii + 1, :], (BF16_ROWS, t)).astype(BF16)
            cnt = pltpu.repeat(cnt, PEER_KEYS // BF16_ROWS, 0)
            e1 = pltpu.repeat(e1, PEER_KEYS // BF16_ROWS, 0)
            term = jnp.where(rk2_ref[h] < cnt, e2_ref[h], jnp.zeros((), BF16)) * e1
            g = term if g is None else g + term
        rows = slice(ii * PEER_KEYS, (ii + 1) * PEER_KEYS)
        w_ref[cur, rows, :] = g * _gelu_tanh(a_ref[rows, :].astype(BF16))

    n_cols = t // MXU_DIM
    ii_per_m = MM_ROWS // PEER_KEYS
    pieces2 = [(m, n) for m in range(d // MM_ROWS) for n in range(n_cols)]
    for n in range(n_cols):
        mm1(0, n)
    for ii in range(n_i):
        if ii % ii_per_m == 0:
            m_next = ii // ii_per_m + 1
            if m_next < ne // MM_ROWS:
                for n in range(n_cols):
                    mm1(m_next, n)
        if (ii * len(pieces2)) % n_i == 0:
            mm2(*pieces2[ii * len(pieces2) // n_i])
        gate_block(ii)

    @pl.when(jnp.logical_and(s >= 1, sb % n_j == n_j - 1))
    def _():
        x2 = x1_ref[...] + acc_ref[...].T
        o_ref[...] = _rms(x2, gfin_ref[...]) if normalise else x2


def _peer(h2, x1, u, vt, cnt, e1, rk2, e2, g_final, tp, ne, normalise):
    n, d = h2.shape
    n_exp = u.shape[0]
    n_i = ne // PEER_KEYS
    n_j = n_exp // ne
    n_steps = (n // tp) * n_j
    assert ne % MM_ROWS == 0 and d % MM_ROWS == 0 and tp % MXU_DIM == 0
    assert (n_i * MXU_DIM * MM_ROWS) % (d * tp) == 0
    sa = lambda s: jnp.minimum(s, n_steps - 1)
    sb = lambda s: jnp.maximum(s - 1, 0)
    tok_a = pl.BlockSpec((tp, d), lambda s: (sa(s) // n_j, 0))
    tok_b = pl.BlockSpec((tp, d), lambda s: (sb(s) // n_j, 0))
    by_i = pl.BlockSpec((PEER_HEADS, n_i, tp), lambda s: (0, sa(s) % n_j, sa(s) // n_j))
    by_j = pl.BlockSpec((PEER_HEADS, PEER_KEYS, tp), lambda s: (0, 0, sa(s) // n_j))
    return pl.pallas_call(
        functools.partial(_peer_kernel, n_j=n_j, normalise=normalise),
        grid=(n_steps + 1,),
        in_specs=[tok_a, tok_b,
                  pl.BlockSpec((ne, d), lambda s: (sa(s) % n_j, 0)),
                  pl.BlockSpec((d, ne), lambda s: (0, sb(s) % n_j)),
                  by_i, by_i, by_j, by_j,
                  pl.BlockSpec(g_final.shape, lambda s: (0, 0))],
        out_specs=tok_b,
        out_shape=jax.ShapeDtypeStruct((n, d), F32),
        scratch_shapes=[pltpu.VMEM((d, tp), F32), pltpu.VMEM((ne, tp), F32),
                        pltpu.VMEM((2, ne, tp), BF16)],
        compiler_params=_params(1),
        name="peer",
    )(h2, x1, u, vt, cnt, e1, rk2, e2, g_final)


def _tiles(batch, seq):
    n = batch * seq
    tm = min(256, seq)
    tf = min(512, n)
    tp = min(512, n)
    return tm, tf, tp


def kernel(x, mem, positions, g_mix, g_mem, w_in, gate_b, attn_sink, conv_w, conv_b, lru_wr, lru_br,
           lru_wi, lru_bi, lru_lambda, w_mem_kv, p_attn, p_lru, p_mem, w_out, g_ffn, w_peer_q,
           peer_sub_keys, peer_u, peer_v, g_final):
    batch, seq, d = x.shape
    mem_len = mem.shape[1]
    depth = w_in.shape[0]
    n = batch * seq
    n_exp = peer_u.shape[1]
    tm, tf, tp = _tiles(batch, seq)
    ne = min(2048, n_exp)

    inv_freq = ROPE_THETA ** (-jnp.arange(0, ROPE_DIM, 2, dtype=F32) / ROPE_DIM)
    invf = jnp.zeros((1, HEAD_DIM), F32).at[0, :ROPE_DIM].set(jnp.tile(inv_freq, 2))
    pos2d = positions.reshape(n, 1)
    offs = [int(o) for o in np.cumsum([0, ATTN_Q_W + 2 * ATTN_KV_W, 2 * LRU_WIDTH, MEM_Q_W])]
    mem2d = mem.reshape(batch * mem_len, d)

    xc = x.reshape(n, d)
    out = None
    for l in range(depth):
        w = w_in[l].astype(BF16)
        wqkv, wlru, wqm, wg = (w[:, offs[0]:offs[1]], w[:, offs[1]:offs[2]], w[:, offs[2]:offs[3]],
                               w[:, offs[3]:])
        q, k, v, xl, gg, qm, gs = _inproj(xc, pos2d, g_mix[l][None], invf, wqkv, wlru, wqm, wg,
                                          gate_b[l][None], tm)
        attn = _attn(attn_sink[l], q, k, v, batch, seq)
        lru = _lru(xl, gg, conv_w[l], conv_b[l][None], lru_wr[l].astype(BF16), lru_br[l],
                   lru_wi[l].astype(BF16), lru_bi[l], lru_lambda[l], batch, seq)
        km, vm = _memkv(mem2d, g_mem[l][None], w_mem_kv[l].astype(BF16), batch, mem_len)
        x1, h2 = _merge(xc, attn, lru, qm, gs, km, vm, p_attn[l].astype(BF16),
                        p_lru[l].astype(BF16), p_mem[l].astype(BF16), w_out[l].astype(BF16),
                        g_ffn[l][None], tm, seq, mem_len)
        cnt, e1, rk2, e2 = _route(h2, w_peer_q[l].astype(BF16), peer_sub_keys[l].astype(BF16), tf)
        last = l == depth - 1
        out = _peer(h2, x1, peer_u[l].astype(BF16), peer_v[l].astype(BF16).T, cnt, e1, rk2, e2,
                    g_final[None], tp, ne, normalise=last)
        xc = out
    return out.reshape(batch, seq, d)
```

```python
import functools
import math

import numpy as np
import jax
import jax.numpy as jnp
from jax import lax
from jax.experimental import pallas as pl
from jax.experimental.pallas import tpu as pltpu

F32 = jnp.float32
BF16 = jnp.bfloat16

HEAD_DIM = 128
ATTN_HEADS = 6
ATTN_KV_HEADS = 2
ATTN_GROUP = ATTN_HEADS // ATTN_KV_HEADS
WINDOW = 128
ROPE_DIM = HEAD_DIM // 4
ROPE_HALF = ROPE_DIM // 2
ROPE_THETA = 500000.0
LRU_WIDTH = 768
LRU_BLOCKS = 6
LRU_BLOCK_W = LRU_WIDTH // LRU_BLOCKS
CONV_WIDTH = 4
CONV_LEFT = CONV_WIDTH // 2
LRU_C = 8.0
MEM_HEADS = 4
N_BRANCH = 3
PEER_HEADS = 8
PEER_KEYS = 128
PEER_HALF = 128
PEER_TOPK = 16
EPS = 1e-6
NEG_INF = -1e30

ATTN_Q_W = ATTN_HEADS * HEAD_DIM
ATTN_KV_W = ATTN_KV_HEADS * HEAD_DIM
MEM_Q_W = MEM_HEADS * HEAD_DIM
ATTN_SCALE = HEAD_DIM ** -0.5

VMEM_LIMIT_BYTES = 56 * 1024 * 1024

F32_ROWS = 8
BF16_ROWS = 16
MXU_DIM = 256
MM_ROWS = 512


def _params(n_axes=1):
    return pltpu.CompilerParams(
        dimension_semantics=("arbitrary",) * n_axes, vmem_limit_bytes=VMEM_LIMIT_BYTES)


def _rms(x, g):
    return x * lax.rsqrt(jnp.mean(x * x, axis=-1, keepdims=True) + EPS) * g


GELU_C0 = 2.0 * math.sqrt(2.0 / math.pi)
GELU_C1 = GELU_C0 * 0.044715


def _gelu_tanh(x):
    return x * jax.nn.sigmoid(x * (GELU_C0 + GELU_C1 * (x * x)))


def _inproj_kernel(x_ref, pos_ref, g_ref, invf_ref, wqkv_ref, wlru_ref, wqm_ref, wg_ref, gb_ref,
                   q_ref, k_ref, v_ref, xl_ref, gg_ref, qm_ref, gs_ref):
    hb = _rms(x_ref[...], g_ref[...]).astype(BF16)
    qkv = jnp.dot(hb, wqkv_ref[...], preferred_element_type=F32)
    ang = pos_ref[...].astype(F32) * invf_ref[...]
    cos = jnp.cos(ang)
    sin = jnp.sin(ang)
    lane = lax.broadcasted_iota(jnp.int32, (1, HEAD_DIM), 1)
    sin_lo = jnp.where(lane < ROPE_HALF, -sin, 0.0)
    sin_hi = jnp.where(lane >= ROPE_HALF, sin, 0.0)

    def rope(t):
        return (t * cos + pltpu.roll(t, HEAD_DIM - ROPE_HALF, 1) * sin_lo
                + pltpu.roll(t, ROPE_HALF, 1) * sin_hi)

    for j in range(ATTN_HEADS):
        sl = slice(j * HEAD_DIM, (j + 1) * HEAD_DIM)
        q_ref[:, sl] = rope(qkv[:, sl]).astype(BF16)
    for j in range(ATTN_KV_HEADS):
        sl = slice(j * HEAD_DIM, (j + 1) * HEAD_DIM)
        k_ref[:, sl] = rope(qkv[:, ATTN_Q_W + j * HEAD_DIM:ATTN_Q_W + (j + 1) * HEAD_DIM]).astype(BF16)
    v_ref[...] = qkv[:, ATTN_Q_W + ATTN_KV_W:].astype(BF16)

    lru = jnp.dot(hb, wlru_ref[...], preferred_element_type=F32)
    xl_ref[...] = lru[:, :LRU_WIDTH]
    gg_ref[...] = jax.nn.gelu(lru[:, LRU_WIDTH:]).astype(BF16)
    qm_ref[...] = jnp.dot(hb, wqm_ref[...], preferred_element_type=F32).astype(BF16)
    gates = jnp.dot(hb, wg_ref[...], preferred_element_type=F32) + gb_ref[...]
    gs_ref[...] = jax.nn.sigmoid(gates).astype(BF16)


def _inproj(x2d, pos2d, g_mix, invf, wqkv, wlru, wqm, wg, gate_b, tm):
    n, d = x2d.shape
    row = lambda w: pl.BlockSpec((tm, w), lambda i: (i, 0))
    full = lambda a: pl.BlockSpec(a.shape, lambda i: (0,) * a.ndim)
    widths = (ATTN_Q_W, ATTN_KV_W, ATTN_KV_W, LRU_WIDTH, LRU_WIDTH, MEM_Q_W, N_BRANCH * d)
    dtypes = (BF16, BF16, BF16, F32, BF16, BF16, BF16)
    return pl.pallas_call(
        _inproj_kernel,
        grid=(n // tm,),
        in_specs=[row(d), row(1), full(g_mix), full(invf), full(wqkv), full(wlru), full(wqm),
                  full(wg), full(gate_b)],
        out_specs=[row(w) for w in widths],
        out_shape=[jax.ShapeDtypeStruct((n, w), t) for w, t in zip(widths, dtypes)],
        compiler_params=_params(),
        name="inproj",
    )(x2d, pos2d, g_mix, invf, wqkv, wlru, wqm, wg, gate_b)


def _attn_kernel(sink_ref, q_ref, k_ref, v_ref, o_ref, *, seq):
    band = 3 * WINDOW
    n_blk = seq // WINDOW

    def block(n, carry):
        qs = pl.multiple_of(n * WINDOW, WINDOW)
        ks = pl.multiple_of(jnp.clip(qs - WINDOW, 0, seq - band), WINDOW)
        q_pos = qs + lax.broadcasted_iota(jnp.int32, (WINDOW, 1), 0)
        k_pos = ks + lax.broadcasted_iota(jnp.int32, (1, band), 1)
        dist = k_pos - q_pos
        valid = jnp.where(dist <= WINDOW, dist, -WINDOW - 1) >= -WINDOW
        for kv in range(ATTN_KV_HEADS):
            kvs = slice(kv * HEAD_DIM, (kv + 1) * HEAD_DIM)
            kb = k_ref[pl.ds(ks, band), kvs]
            vb = v_ref[pl.ds(ks, band), kvs]
            for g in range(ATTN_GROUP):
                h = kv * ATTN_GROUP + g
                hs = slice(h * HEAD_DIM, (h + 1) * HEAD_DIM)
                s = lax.dot_general(q_ref[pl.ds(qs, WINDOW), hs], kb, (((1,), (1,)), ((), ())),
                                    preferred_element_type=F32) * ATTN_SCALE
                s = jnp.where(valid, s, NEG_INF)
                sink = sink_ref[h]
                m = jnp.maximum(jnp.max(s, axis=-1, keepdims=True), sink)
                p = jnp.exp(s - m)
                denom = jnp.sum(p, axis=-1, keepdims=True) + jnp.exp(sink - m)
                o = jnp.dot(p.astype(BF16), vb, preferred_element_type=F32) / denom
                o_ref[pl.ds(qs, WINDOW), hs] = o.astype(BF16)
        return carry

    lax.fori_loop(0, n_blk, block, 0)


def _attn(sink, q, k, v, batch, seq):
    spec = lambda w: pl.BlockSpec((seq, w), lambda b: (b, 0))
    return pl.pallas_call(
        functools.partial(_attn_kernel, seq=seq),
        grid=(batch,),
        in_specs=[pl.BlockSpec(memory_space=pltpu.SMEM), spec(ATTN_Q_W), spec(ATTN_KV_W),
                  spec(ATTN_KV_W)],
        out_specs=spec(ATTN_Q_W),
        out_shape=jax.ShapeDtypeStruct((batch * seq, ATTN_Q_W), BF16),
        compiler_params=_params(),
        name="attn",
    )(sink, q, k, v)


def _scan(a, b, t_idx, seq, reverse):
    s = 1
    while s < seq:
        if reverse:
            ok = t_idx < seq - s
            a_n = pltpu.roll(a, seq - s, 0)
            b_n = pltpu.roll(b, seq - s, 0)
        else:
            ok = t_idx >= s
            a_n = pltpu.roll(a, s, 0)
            b_n = pltpu.roll(b, s, 0)
        b = jnp.where(ok, a * b_n + b, b)
        a = jnp.where(ok, a * a_n, a)
        s *= 2
    return b


def _lru_kernel(xl_ref, gg_ref, cw_ref, cb_ref, wr_ref, br_ref, wi_ref, bi_ref, lam_ref, o_ref,
                *, seq):
    t_idx = lax.broadcasted_iota(jnp.int32, (seq, LRU_BLOCK_W), 0)
    for c in range(LRU_BLOCKS):
        cs = slice(c * LRU_BLOCK_W, (c + 1) * LRU_BLOCK_W)
        x = xl_ref[:, cs]
        xc = cb_ref[:, cs] + cw_ref[2:3, cs] * x
        xc = xc + cw_ref[0:1, cs] * jnp.where(t_idx >= 2, pltpu.roll(x, 2, 0), 0.0)
        xc = xc + cw_ref[1:2, cs] * jnp.where(t_idx >= 1, pltpu.roll(x, 1, 0), 0.0)
        xc = xc + cw_ref[3:4, cs] * jnp.where(t_idx < seq - 1, pltpu.roll(x, seq - 1, 0), 0.0)
        xcb = xc.astype(BF16)
        h = None
        for d in range(2):
            r = jax.nn.sigmoid(jnp.dot(xcb, wr_ref[d, c], preferred_element_type=F32)
                               + br_ref[d:d + 1, cs])
            i = jax.nn.sigmoid(jnp.dot(xcb, wi_ref[d, c], preferred_element_type=F32)
                               + bi_ref[d:d + 1, cs])
            z = -lam_ref[d:d + 1, cs]
            softplus = jnp.maximum(z, 0.0) + jnp.log1p(jnp.exp(-jnp.abs(z)))
            log_a = (-LRU_C) * r * softplus
            a = jnp.exp(log_a)
            bx = jnp.sqrt(-jnp.tanh(log_a) * (a * a + 1.0)) * (i * xc)
            hd = _scan(a, bx, t_idx, seq, reverse=(d == 1))
            h = hd if h is None else h + hd
        o_ref[:, cs] = (h * gg_ref[:, cs].astype(F32)).astype(BF16)


def _lru(xl, gg, conv_w, conv_b, wr, br, wi, bi, lam, batch, seq):
    spec = pl.BlockSpec((seq, LRU_WIDTH), lambda b: (b, 0))
    full = lambda a: pl.BlockSpec(a.shape, lambda b: (0,) * a.ndim)
    return pl.pallas_call(
        functools.partial(_lru_kernel, seq=seq),
        grid=(batch,),
        in_specs=[spec, spec, full(conv_w), full(conv_b), full(wr), full(br), full(wi), full(bi),
                  full(lam)],
        out_specs=spec,
        out_shape=jax.ShapeDtypeStruct((batch * seq, LRU_WIDTH), BF16),
        compiler_params=_params(),
        name="lru",
    )(xl, gg, conv_w, conv_b, wr, br, wi, bi, lam)


def _memkv_kernel(mem_ref, g_ref, w_ref, k_ref, v_ref):
    hb = _rms(mem_ref[...], g_ref[...]).astype(BF16)
    kv = jnp.dot(hb, w_ref[...], preferred_element_type=F32)
    k_ref[...] = kv[:, :MEM_Q_W].astype(BF16)
    v_ref[...] = kv[:, MEM_Q_W:].astype(BF16)


def _memkv(mem2d, g_mem, w, batch, mem_len):
    d = mem2d.shape[1]
    full = lambda a: pl.BlockSpec(a.shape, lambda b: (0,) * a.ndim)
    out = pl.BlockSpec((mem_len, MEM_Q_W), lambda b: (b, 0))
    return pl.pallas_call(
        _memkv_kernel,
        grid=(batch,),
        in_specs=[pl.BlockSpec((mem_len, d), lambda b: (b, 0)), full(g_mem), full(w)],
        out_specs=[out, out],
        out_shape=[jax.ShapeDtypeStruct((batch * mem_len, MEM_Q_W), BF16)] * 2,
        compiler_params=_params(),
        name="memkv",
    )(mem2d, g_mem, w)


def _merge_kernel(x_ref, at_ref, lr_ref, qm_ref, gs_ref, km_ref, vm_ref, pa_ref, pl_ref, pm_ref,
                  wo_ref, gf_ref, x1_ref, h2_ref, *, d):
    mem_heads = []
    for h in range(MEM_HEADS):
        hs = slice(h * HEAD_DIM, (h + 1) * HEAD_DIM)
        s = lax.dot_general(qm_ref[:, hs], km_ref[:, hs], (((1,), (1,)), ((), ())),
                            preferred_element_type=F32) * ATTN_SCALE
        p = jnp.exp(s - jnp.max(s, axis=-1, keepdims=True))
        o = jnp.dot(p.astype(BF16), vm_ref[:, hs], preferred_element_type=F32)
        mem_heads.append((o / jnp.sum(p, axis=-1, keepdims=True)).astype(BF16))
    mem_out = jnp.concatenate(mem_heads, axis=-1)
    merged = gs_ref[:, 0:d].astype(F32) * jnp.dot(at_ref[...], pa_ref[...],
                                                  preferred_element_type=F32)
    merged += gs_ref[:, d:2 * d].astype(F32) * jnp.dot(lr_ref[...], pl_ref[...],
                                                       preferred_element_type=F32)
    merged += gs_ref[:, 2 * d:3 * d].astype(F32) * jnp.dot(mem_out, pm_ref[...],
                                                           preferred_element_type=F32)
    x1 = x_ref[...] + jnp.dot(merged.astype(BF16), wo_ref[...], preferred_element_type=F32)
    x1_ref[...] = x1
    h2_ref[...] = _rms(x1, gf_ref[...]).astype(BF16)


def _merge(x2d, attn, lru, qm, gs, km, vm, pa, pl_w, pm, wo, g_ffn, tm, seq, mem_len):
    n, d = x2d.shape
    per_seq = seq // tm
    row = lambda w: pl.BlockSpec((tm, w), lambda i: (i, 0))
    full = lambda a: pl.BlockSpec(a.shape, lambda i: (0,) * a.ndim)
    memspec = pl.BlockSpec((mem_len, MEM_Q_W), lambda i: (i // per_seq, 0))
    return pl.pallas_call(
        functools.partial(_merge_kernel, d=d),
        grid=(n // tm,),
        in_specs=[row(d), row(ATTN_Q_W), row(LRU_WIDTH), row(MEM_Q_W), row(N_BRANCH * d), memspec,
                  memspec, full(pa), full(pl_w), full(pm), full(wo), full(g_ffn)],
        out_specs=[row(d), row(d)],
        out_shape=[jax.ShapeDtypeStruct((n, d), F32), jax.ShapeDtypeStruct((n, d), BF16)],
        compiler_params=_params(),
        name="merge",
    )(x2d, attn, lru, qm, gs, km, vm, pa, pl_w, pm, wo, g_ffn)


def _allsub(x, op):
    for shift in (4, 2, 1):
        x = op(x, pltpu.roll(x, shift, 0))
    return x


def _tree(xs, op):
    xs = list(xs)
    while len(xs) > 1:
        nxt = [op(xs[i], xs[i + 1]) for i in range(0, len(xs) - 1, 2)]
        if len(xs) % 2:
            nxt.append(xs[-1])
        xs = nxt
    return xs[0]


def _extract16(work, sub, rank=None):
    n_tiles = len(work)
    vals, knocks = [], []
    for k in range(PEER_TOPK):
        m = _allsub(_tree(work, jnp.maximum), jnp.maximum)
        chains = []
        for c0 in range(0, n_tiles, 4):
            acc = float(n_tiles)
            for r in reversed(range(c0, min(c0 + 4, n_tiles))):
                acc = jnp.where(work[r] == m, float(r), acc)
            chains.append(acc)
        first = _allsub(_tree(chains, jnp.minimum) * F32_ROWS + sub, jnp.minimum)
        first_tile = jnp.floor(first * (1.0 / F32_ROWS))
        knock = jnp.where(sub == first - first_tile * F32_ROWS, first_tile, -1.0)
        for r in range(n_tiles):
            hit = knock == float(r)
            work[r] = jnp.where(hit, -jnp.inf, work[r])
            if rank is not None:
                rank[r] = jnp.where(hit, float(k), rank[r])
        vals.append(m)
        knocks.append(knock)
    return vals, knocks


def _route_kernel(h2_ref, wq_ref, key_ref, cnt_ref, e1_ref, rk2_ref, e2_ref):
    assert PEER_TOPK == 16 and PEER_KEYS % F32_ROWS == 0
    t = h2_ref.shape[0]
    n_tiles = PEER_KEYS // F32_ROWS
    q = jnp.dot(h2_ref[...], wq_ref[...], preferred_element_type=F32).astype(BF16)
    nt = (((1,), (1,)), ((), ()))
    s1 = lax.dot_general(key_ref[0, 0], q[:, :PEER_HALF], nt, preferred_element_type=F32)
    s2 = lax.dot_general(key_ref[0, 1], q[:, PEER_HALF:], nt, preferred_element_type=F32)
    tiles = lambda s: [s[r * F32_ROWS:(r + 1) * F32_ROWS] for r in range(n_tiles)]
    sub = lax.broadcasted_iota(jnp.int32, (F32_ROWS, t), 0).astype(F32)
    neg = -jnp.inf

    s1t, s2t = tiles(s1), tiles(s2)
    work1, work2 = list(s1t), list(s2t)
    rank2 = [jnp.full((F32_ROWS, t), float(PEER_TOPK), F32)] * n_tiles
    a, knock1 = _extract16(work1, sub)
    b, _ = _extract16(work2, sub, rank2)

    def spread(vs):
        out = vs[0]
        for i in range(1, F32_ROWS):
            out = jnp.where(sub == float(i), vs[i], out)
        return out

    b_lo, b_hi, a_hi = spread(b[:F32_ROWS]), spread(b[F32_ROWS:]), spread(a[F32_ROWS:])
    b_r2, b_r4 = pltpu.roll(b_lo, 2, 0), pltpu.roll(b_lo, 4, 0)
    cand = [
        a[0] + b_lo,
        a[0] + b_hi,
        a[1] + b_lo,
        jnp.where(sub < 5.0, a[2] + b_lo, neg),
        jnp.where(sub < 4.0, a[3] + b_lo, jnp.where(sub < 7.0, a[4] + b_r4, neg)),
        jnp.where(sub < 2.0, a[5] + b_lo,
                  jnp.where(sub < 4.0, a[6] + b_r2, jnp.where(sub < 6.0, a[7] + b_r4, neg))),
        a_hi + b[0],
    ]
    work = list(cand)
    _extract16(work, sub)
    sel = [jnp.where(w < c, 1.0, 0.0) for w, c in zip(work, cand)]
    v0 = a[0] + b[0]
    z = _allsub(_tree([s * jnp.exp(c - v0) for s, c in zip(sel, cand)], jnp.add), jnp.add)

    count = lambda x: _allsub(x, jnp.add)
    c = [count(sel[0] + sel[1]), count(sel[2]), count(sel[3]),
         count(jnp.where(sub < 4.0, sel[4], 0.0)), count(jnp.where(sub < 4.0, 0.0, sel[4])),
         count(jnp.where(sub < 2.0, sel[5], 0.0)),
         count(jnp.where(sub < 2.0, 0.0, jnp.where(sub < 4.0, sel[5], 0.0))),
         count(jnp.where(sub < 4.0, 0.0, sel[5]))]
    c += [count(jnp.where(sub == float(i), sel[6], 0.0)) for i in range(F32_ROWS)]

    inv_z = 1.0 / z
    for r in range(n_tiles):
        rows = slice(r * F32_ROWS, (r + 1) * F32_ROWS)
        cnt = jnp.zeros((F32_ROWS, t), F32)
        for p in range(PEER_TOPK):
            cnt = jnp.where(knock1[p] == float(r), c[p], cnt)
        cnt_ref[0, rows, :] = cnt
        e1_ref[0, rows, :] = jnp.where(work1[r] == neg, jnp.exp(s1t[r] - a[0]), 0.0)
    for r in range(0, n_tiles, 2):
        rows = slice(r * F32_ROWS, (r + 2) * F32_ROWS)
        rk = jnp.concatenate([rank2[r], rank2[r + 1]], axis=0)
        ex = jnp.concatenate([jnp.exp(s2t[r] - b[0]), jnp.exp(s2t[r + 1] - b[0])], axis=0)
        inv = jnp.concatenate([inv_z, inv_z], axis=0)
        rk2_ref[0, rows, :] = rk.astype(BF16)
        e2_ref[0, rows, :] = jnp.where(rk < PEER_TOPK, ex * inv, 0.0).astype(BF16)


def _route(h2, wq, keys, tf):
    n, d = h2.shape
    by_i = pl.BlockSpec((1, PEER_KEYS, tf), lambda i, h: (h, 0, i))
    by_j = by_i
    by_i_shape = jax.ShapeDtypeStruct((PEER_HEADS, PEER_KEYS, n), F32)
    by_j_shape = jax.ShapeDtypeStruct((PEER_HEADS, PEER_KEYS, n), BF16)
    return pl.pallas_call(
        _route_kernel,
        grid=(n // tf, PEER_HEADS),
        in_specs=[pl.BlockSpec((tf, d), lambda i, h: (i, 0)),
                  pl.BlockSpec((d, 2 * PEER_HALF), lambda i, h: (0, h)),
                  pl.BlockSpec((1, 2, PEER_KEYS, PEER_HALF), lambda i, h: (h, 0, 0, 0))],
        out_specs=[by_i, by_i, by_j, by_j],
        out_shape=[by_i_shape, by_i_shape, by_j_shape, by_j_shape],
        compiler_params=_params(2),
        name="route",
    )(h2, wq, keys)


def _peer_kernel(h2_ref, x1_ref, u_ref, vt_ref, cnt_ref, e1_ref, rk2_ref, e2_ref, gfin_ref,
                 o_ref, acc_ref, a_ref, w_ref, *, n_j, normalise):
    s = pl.program_id(0)
    sb = jnp.maximum(s - 1, 0)
    cur = s % 2
    prev = 1 - cur
    ne, t = a_ref.shape
    d = acc_ref.shape[0]
    n_i = ne // PEER_KEYS

    @pl.when(s == 0)
    def _():
        w_ref[...] = jnp.zeros_like(w_ref)

    @pl.when(sb % n_j == 0)
    def _():
        acc_ref[...] = jnp.zeros_like(acc_ref)

    def mm1(m, n):
        rows = slice(m * MM_ROWS, (m + 1) * MM_ROWS)
        cols = slice(n * MXU_DIM, (n + 1) * MXU_DIM)
        a_ref[rows, cols] = lax.dot_general(u_ref[rows, :], h2_ref[cols, :],
                                            (((1,), (1,)), ((), ())), preferred_element_type=F32)

    def mm2(m, n):
        rows = slice(m * MM_ROWS, (m + 1) * MM_ROWS)
        cols = slice(n * MXU_DIM, (n + 1) * MXU_DIM)
        acc_ref[rows, cols] += jnp.dot(vt_ref[rows, :], w_ref[prev, :, cols],
                                       preferred_element_type=F32)

    def gate_block(ii):
        g = None
        for h in range(PEER_HEADS):
            cnt = jnp.broadcast_to(cnt_ref[h, ii:ii + 1, :], (BF16_ROWS, t)).astype(BF16)
            e1 = jnp.broadcast_to(e1_ref[h, ii:ii + 1, :], (BF16_ROWS, t)).astype(BF16)
            cnt = pltpu.repeat(cnt, PEER_KEYS // BF16_ROWS, 0)
            e1 = pltpu.repeat(e1, PEER_KEYS // BF16_ROWS, 0)
            term = jnp.where(rk2_ref[h] < cnt, e2_ref[h], jnp.zeros((), BF16)) * e1
            g = term if g is None else g + term
        rows = slice(ii * PEER_KEYS, (ii + 1) * PEER_KEYS)
        w_ref[cur, rows, :] = g * _gelu_tanh(a_ref[rows, :].astype(BF16))

    n_cols = t // MXU_DIM
    ii_per_m = MM_ROWS // PEER_KEYS
    pieces2 = [(m, n) for m in range(d // MM_ROWS) for n in range(n_cols)]
    for n in range(n_cols):
        mm1(0, n)
    for ii in range(n_i):
        if ii % ii_per_m == 0:
            m_next = ii // ii_per_m + 1
            if m_next < ne // MM_ROWS:
                for n in range(n_cols):
                    mm1(m_next, n)
        if (ii * len(pieces2)) % n_i == 0:
            mm2(*pieces2[ii * len(pieces2) // n_i])
        gate_block(ii)

    @pl.when(jnp.logical_and(s >= 1, sb % n_j == n_j - 1))
    def _():
        x2 = x1_ref[...] + acc_ref[...].T
        o_ref[...] = _rms(x2, gfin_ref[...]) if normalise else x2


def _peer(h2, x1, u, vt, cnt, e1, rk2, e2, g_final, tp, ne, normalise):
    n, d = h2.shape
    n_exp = u.shape[0]
    n_i = ne // PEER_KEYS
    n_j = n_exp // ne
    n_steps = (n // tp) * n_j
    assert ne % MM_ROWS == 0 and d % MM_ROWS == 0 and tp % MXU_DIM == 0
    assert (n_i * MXU_DIM * MM_ROWS) % (d * tp) == 0
    sa = lambda s: jnp.minimum(s, n_steps - 1)
    sb = lambda s: jnp.maximum(s - 1, 0)
    tok_a = pl.BlockSpec((tp, d), lambda s: (sa(s) // n_j, 0))
    tok_b = pl.BlockSpec((tp, d), lambda s: (sb(s) // n_j, 0))
    by_i = pl.BlockSpec((PEER_HEADS, n_i, tp), lambda s: (0, sa(s) % n_j, sa(s) // n_j))
    by_j = pl.BlockSpec((PEER_HEADS, PEER_KEYS, tp), lambda s: (0, 0, sa(s) // n_j))
    return pl.pallas_call(
        functools.partial(_peer_kernel, n_j=n_j, normalise=normalise),
        grid=(n_steps + 1,),
        in_specs=[tok_a, tok_b,
                  pl.BlockSpec((ne, d), lambda s: (sa(s) % n_j, 0)),
                  pl.BlockSpec((d, ne), lambda s: (0, sb(s) % n_j)),
                  by_i, by_i, by_j, by_j,
                  pl.BlockSpec(g_final.shape, lambda s: (0, 0))],
        out_specs=tok_b,
        out_shape=jax.ShapeDtypeStruct((n, d), F32),
        scratch_shapes=[pltpu.VMEM((d, tp), F32), pltpu.VMEM((ne, tp), F32),
                        pltpu.VMEM((2, ne, tp), BF16)],
        compiler_params=_params(1),
        name="peer",
    )(h2, x1, u, vt, cnt, e1, rk2, e2, g_final)


def _tiles(batch, seq):
    n = batch * seq
    tm = min(256, seq)
    tf = min(512, n)
    tp = min(512, n)
    return tm, tf, tp


def kernel(x, mem, positions, g_mix, g_mem, w_in, gate_b, attn_sink, conv_w, conv_b, lru_wr, lru_br,
           lru_wi, lru_bi, lru_lambda, w_mem_kv, p_attn, p_lru, p_mem, w_out, g_ffn, w_peer_q,
           peer_sub_keys, peer_u, peer_v, g_final):
    batch, seq, d = x.shape
    mem_len = mem.shape[1]
    depth = w_in.shape[0]
    n = batch * seq
    n_exp = peer_u.shape[1]
    tm, tf, tp = _tiles(batch, seq)
    ne = min(2048, n_exp)

    inv_freq = ROPE_THETA ** (-jnp.arange(0, ROPE_DIM, 2, dtype=F32) / ROPE_DIM)
    invf = jnp.zeros((1, HEAD_DIM), F32).at[0, :ROPE_DIM].set(jnp.tile(inv_freq, 2))
    pos2d = positions.reshape(n, 1)
    offs = [int(o) for o in np.cumsum([0, ATTN_Q_W + 2 * ATTN_KV_W, 2 * LRU_WIDTH, MEM_Q_W])]
    mem2d = mem.reshape(batch * mem_len, d)

    xc = x.reshape(n, d)
    out = None
    for l in range(depth):
        w = w_in[l].astype(BF16)
        wqkv, wlru, wqm, wg = (w[:, offs[0]:offs[1]], w[:, offs[1]:offs[2]], w[:, offs[2]:offs[3]],
                               w[:, offs[3]:])
        q, k, v, xl, gg, qm, gs = _inproj(xc, pos2d, g_mix[l][None], invf, wqkv, wlru, wqm, wg,
                                          gate_b[l][None], tm)
        attn = _attn(attn_sink[l], q, k, v, batch, seq)
        lru = _lru(xl, gg, conv_w[l], conv_b[l][None], lru_wr[l].astype(BF16), lru_br[l],
                   lru_wi[l].astype(BF16), lru_bi[l], lru_lambda[l], batch, seq)
        km, vm = _memkv(mem2d, g_mem[l][None], w_mem_kv[l].astype(BF16), batch, mem_len)
        x1, h2 = _merge(xc, attn, lru, qm, gs, km, vm, p_attn[l].astype(BF16),
                        p_lru[l].astype(BF16), p_mem[l].astype(BF16), w_out[l].astype(BF16),
                        g_ffn[l][None], tm, seq, mem_len)
        cnt, e1, rk2, e2 = _route(h2, w_peer_q[l].astype(BF16), peer_sub_keys[l].astype(BF16), tf)
        last = l == depth - 1
        out = _peer(h2, x1, peer_u[l].astype(BF16), peer_v[l].astype(BF16).T, cnt, e1, rk2, e2,
                    g_final[None], tp, ne, normalise=last)
        xc = out
    return out.reshape(batch, seq, d)
```

```python
import functools
import math

import numpy as np
import jax
import jax.numpy as jnp
from jax import lax
from jax.experimental import pallas as pl
from jax.experimental.pallas import tpu as pltpu

F32 = jnp.float32
BF16 = jnp.bfloat16

HEAD_DIM = 128
ATTN_HEADS = 6
ATTN_KV_HEADS = 2
ATTN_GROUP = ATTN_HEADS // ATTN_KV_HEADS
WINDOW = 128
ROPE_DIM = HEAD_DIM // 4
ROPE_HALF = ROPE_DIM // 2
ROPE_THETA = 500000.0
LRU_WIDTH = 768
LRU_BLOCKS = 6
LRU_BLOCK_W = LRU_WIDTH // LRU_BLOCKS
CONV_WIDTH = 4
CONV_LEFT = CONV_WIDTH // 2
LRU_C = 8.0
MEM_HEADS = 4
N_BRANCH = 3
PEER_HEADS = 8
PEER_KEYS = 128
PEER_HALF = 128
PEER_TOPK = 16
EPS = 1e-6
NEG_INF = -1e30

ATTN_Q_W = ATTN_HEADS * HEAD_DIM
ATTN_KV_W = ATTN_KV_HEADS * HEAD_DIM
MEM_Q_W = MEM_HEADS * HEAD_DIM
ATTN_SCALE = HEAD_DIM ** -0.5

VMEM_LIMIT_BYTES = 56 * 1024 * 1024

F32_ROWS = 8
BF16_ROWS = 16
MXU_DIM = 256
MM_ROWS = 512
ROUTE_HEADS_PER_STEP = 4


def _params(n_axes=1):
    return pltpu.CompilerParams(
        dimension_semantics=("arbitrary",) * n_axes, vmem_limit_bytes=VMEM_LIMIT_BYTES)


def _rms(x, g):
    return x * lax.rsqrt(jnp.mean(x * x, axis=-1, keepdims=True) + EPS) * g


GELU_C0 = 2.0 * math.sqrt(2.0 / math.pi)
GELU_C1 = GELU_C0 * 0.044715


def _gelu_tanh(x):
    return x * jax.nn.sigmoid(x * (GELU_C0 + GELU_C1 * (x * x)))


def _inproj_kernel(x_ref, pos_ref, g_ref, invf_ref, wqkv_ref, wlru_ref, wqm_ref, wg_ref, gb_ref,
                   q_ref, k_ref, v_ref, xl_ref, gg_ref, qm_ref, gs_ref):
    hb = _rms(x_ref[...], g_ref[...]).astype(BF16)
    qkv = jnp.dot(hb, wqkv_ref[...], preferred_element_type=F32)
    ang = pos_ref[...].astype(F32) * invf_ref[...]
    cos = jnp.cos(ang)
    sin = jnp.sin(ang)
    lane = lax.broadcasted_iota(jnp.int32, (1, HEAD_DIM), 1)
    sin_lo = jnp.where(lane < ROPE_HALF, -sin, 0.0)
    sin_hi = jnp.where(lane >= ROPE_HALF, sin, 0.0)

    def rope(t):
        return (t * cos + pltpu.roll(t, HEAD_DIM - ROPE_HALF, 1) * sin_lo
                + pltpu.roll(t, ROPE_HALF, 1) * sin_hi)

    for j in range(ATTN_HEADS):
        sl = slice(j * HEAD_DIM, (j + 1) * HEAD_DIM)
        q_ref[:, sl] = rope(qkv[:, sl]).astype(BF16)
    for j in range(ATTN_KV_HEADS):
        sl = slice(j * HEAD_DIM, (j + 1) * HEAD_DIM)
        k_ref[:, sl] = rope(qkv[:, ATTN_Q_W + j * HEAD_DIM:ATTN_Q_W + (j + 1) * HEAD_DIM]).astype(BF16)
    v_ref[...] = qkv[:, ATTN_Q_W + ATTN_KV_W:].astype(BF16)

    lru = jnp.dot(hb, wlru_ref[...], preferred_element_type=F32)
    xl_ref[...] = lru[:, :LRU_WIDTH]
    gg_ref[...] = jax.nn.gelu(lru[:, LRU_WIDTH:]).astype(BF16)
    qm_ref[...] = jnp.dot(hb, wqm_ref[...], preferred_element_type=F32).astype(BF16)
    gates = jnp.dot(hb, wg_ref[...], preferred_element_type=F32) + gb_ref[...]
    gs_ref[...] = jax.nn.sigmoid(gates).astype(BF16)


def _inproj(x2d, pos2d, g_mix, invf, wqkv, wlru, wqm, wg, gate_b, tm):
    n, d = x2d.shape
    row = lambda w: pl.BlockSpec((tm, w), lambda i: (i, 0))
    full = lambda a: pl.BlockSpec(a.shape, lambda i: (0,) * a.ndim)
    widths = (ATTN_Q_W, ATTN_KV_W, ATTN_KV_W, LRU_WIDTH, LRU_WIDTH, MEM_Q_W, N_BRANCH * d)
    dtypes = (BF16, BF16, BF16, F32, BF16, BF16, BF16)
    return pl.pallas_call(
        _inproj_kernel,
        grid=(n // tm,),
        in_specs=[row(d), row(1), full(g_mix), full(invf), full(wqkv), full(wlru), full(wqm),
                  full(wg), full(gate_b)],
        out_specs=[row(w) for w in widths],
        out_shape=[jax.ShapeDtypeStruct((n, w), t) for w, t in zip(widths, dtypes)],
        compiler_params=_params(),
        name="inproj",
    )(x2d, pos2d, g_mix, invf, wqkv, wlru, wqm, wg, gate_b)


def _attn_kernel(sink_ref, q_ref, k_ref, v_ref, o_ref, *, seq):
    band = 3 * WINDOW
    masks = {}
    for n in range(seq // WINDOW):
        qs = n * WINDOW
        ks = min(max(qs - WINDOW, 0), seq - band)
        if ks - qs not in masks:
            dist = (ks - qs) + (lax.broadcasted_iota(jnp.int32, (WINDOW, band), 1)
                                - lax.broadcasted_iota(jnp.int32, (WINDOW, band), 0))
            masks[ks - qs] = jnp.where(dist <= WINDOW, dist, -WINDOW - 1) >= -WINDOW
        valid = masks[ks - qs]
        for kv in range(ATTN_KV_HEADS):
            kvs = slice(kv * HEAD_DIM, (kv + 1) * HEAD_DIM)
            kb = k_ref[ks:ks + band, kvs]
            vb = v_ref[ks:ks + band, kvs]
            heads = [kv * ATTN_GROUP + g for g in range(ATTN_GROUP)]
            q3 = jnp.concatenate(
                [q_ref[qs:qs + WINDOW, h * HEAD_DIM:(h + 1) * HEAD_DIM] for h in heads], axis=0)
            s3 = lax.dot_general(q3, kb, (((1,), (1,)), ((), ())),
                                 preferred_element_type=F32) * ATTN_SCALE
            ps, denoms = [], []
            for g, h in enumerate(heads):
                s = jnp.where(valid, s3[g * WINDOW:(g + 1) * WINDOW], NEG_INF)
                sink = sink_ref[h]
                m = jnp.maximum(jnp.max(s, axis=-1, keepdims=True), sink)
                p = jnp.exp(s - m)
                denoms.append(jnp.sum(p, axis=-1, keepdims=True) + jnp.exp(sink - m))
                ps.append(p.astype(BF16))
            o3 = jnp.dot(jnp.concatenate(ps, axis=0), vb, preferred_element_type=F32)
            for g, h in enumerate(heads):
                o = o3[g * WINDOW:(g + 1) * WINDOW] / denoms[g]
                o_ref[qs:qs + WINDOW, h * HEAD_DIM:(h + 1) * HEAD_DIM] = o.astype(BF16)


def _attn(sink, q, k, v, batch, seq):
    spec = lambda w: pl.BlockSpec((seq, w), lambda b: (b, 0))
    return pl.pallas_call(
        functools.partial(_attn_kernel, seq=seq),
        grid=(batch,),
        in_specs=[pl.BlockSpec(memory_space=pltpu.SMEM), spec(ATTN_Q_W), spec(ATTN_KV_W),
                  spec(ATTN_KV_W)],
        out_specs=spec(ATTN_Q_W),
        out_shape=jax.ShapeDtypeStruct((batch * seq, ATTN_Q_W), BF16),
        compiler_params=_params(),
        name="attn",
    )(sink, q, k, v)


def _scan(a, b, t_idx, seq, reverse):
    s = 1
    while s < seq:
        if s % F32_ROWS:
            if reverse:
                ok = t_idx < seq - s
                a_n = pltpu.roll(a, seq - s, 0)
                b_n = pltpu.roll(b, seq - s, 0)
            else:
                ok = t_idx >= s
                a_n = pltpu.roll(a, s, 0)
                b_n = pltpu.roll(b, s, 0)
            b = jnp.where(ok, a * b_n + b, b)
            a = jnp.where(ok, a * a_n, a)
        else:
            pad = (s, a.shape[1])
            if reverse:
                a_n = jnp.concatenate([a[s:], jnp.ones(pad, F32)], axis=0)
                b_n = jnp.concatenate([b[s:], jnp.zeros(pad, F32)], axis=0)
            else:
                a_n = jnp.concatenate([jnp.ones(pad, F32), a[:seq - s]], axis=0)
                b_n = jnp.concatenate([jnp.zeros(pad, F32), b[:seq - s]], axis=0)
            b = a * b_n + b
            if 2 * s < seq:
                a = a * a_n
        s *= 2
    return b


def _lru_kernel(xl_ref, gg_ref, cw_ref, cb_ref, wr_ref, br_ref, wi_ref, bi_ref, lam_ref, o_ref,
                *, seq):
    t_idx = lax.broadcasted_iota(jnp.int32, (seq, LRU_BLOCK_W), 0)
    for c in range(LRU_BLOCKS):
        cs = slice(c * LRU_BLOCK_W, (c + 1) * LRU_BLOCK_W)
        x = xl_ref[:, cs]
        xc = cb_ref[:, cs] + cw_ref[2:3, cs] * x
        xc = xc + cw_ref[0:1, cs] * jnp.where(t_idx >= 2, pltpu.roll(x, 2, 0), 0.0)
        xc = xc + cw_ref[1:2, cs] * jnp.where(t_idx >= 1, pltpu.roll(x, 1, 0), 0.0)
        xc = xc + cw_ref[3:4, cs] * jnp.where(t_idx < seq - 1, pltpu.roll(x, seq - 1, 0), 0.0)
        xcb = xc.astype(BF16)
        h = None
        for d in range(2):
            r = jax.nn.sigmoid(jnp.dot(xcb, wr_ref[d, c], preferred_element_type=F32)
                               + br_ref[d:d + 1, cs])
            i = jax.nn.sigmoid(jnp.dot(xcb, wi_ref[d, c], preferred_element_type=F32)
                               + bi_ref[d:d + 1, cs])
            z = -lam_ref[d:d + 1, cs]
            softplus = jnp.maximum(z, 0.0) + jnp.log1p(jnp.exp(-jnp.abs(z)))
            log_a = (-LRU_C) * r * softplus
            a = jnp.exp(log_a)
            bx = jnp.sqrt(-jnp.tanh(log_a) * (a * a + 1.0)) * (i * xc)
            hd = _scan(a, bx, t_idx, seq, reverse=(d == 1))
            h = hd if h is None else h + hd
        o_ref[:, cs] = (h * gg_ref[:, cs].astype(F32)).astype(BF16)


def _lru(xl, gg, conv_w, conv_b, wr, br, wi, bi, lam, batch, seq):
    spec = pl.BlockSpec((seq, LRU_WIDTH), lambda b: (b, 0))
    full = lambda a: pl.BlockSpec(a.shape, lambda b: (0,) * a.ndim)
    return pl.pallas_call(
        functools.partial(_lru_kernel, seq=seq),
        grid=(batch,),
        in_specs=[spec, spec, full(conv_w), full(conv_b), full(wr), full(br), full(wi), full(bi),
                  full(lam)],
        out_specs=spec,
        out_shape=jax.ShapeDtypeStruct((batch * seq, LRU_WIDTH), BF16),
        compiler_params=_params(),
        name="lru",
    )(xl, gg, conv_w, conv_b, wr, br, wi, bi, lam)


def _memkv_kernel(mem_ref, g_ref, w_ref, k_ref, v_ref):
    hb = _rms(mem_ref[...], g_ref[...]).astype(BF16)
    kv = jnp.dot(hb, w_ref[...], preferred_element_type=F32)
    k_ref[...] = kv[:, :MEM_Q_W].astype(BF16)
    v_ref[...] = kv[:, MEM_Q_W:].astype(BF16)


def _memkv(mem2d, g_mem, w, batch, mem_len):
    d = mem2d.shape[1]
    full = lambda a: pl.BlockSpec(a.shape, lambda b: (0,) * a.ndim)
    out = pl.BlockSpec((mem_len, MEM_Q_W), lambda b: (b, 0))
    return pl.pallas_call(
        _memkv_kernel,
        grid=(batch,),
        in_specs=[pl.BlockSpec((mem_len, d), lambda b: (b, 0)), full(g_mem), full(w)],
        out_specs=[out, out],
        out_shape=[jax.ShapeDtypeStruct((batch * mem_len, MEM_Q_W), BF16)] * 2,
        compiler_params=_params(),
        name="memkv",
    )(mem2d, g_mem, w)


def _merge_kernel(x_ref, at_ref, lr_ref, qm_ref, gs_ref, km_ref, vm_ref, pa_ref, pl_ref, pm_ref,
                  wo_ref, gf_ref, x1_ref, h2_ref, *, d):
    mem_heads = []
    for h in range(MEM_HEADS):
        hs = slice(h * HEAD_DIM, (h + 1) * HEAD_DIM)
        s = lax.dot_general(qm_ref[:, hs], km_ref[:, hs], (((1,), (1,)), ((), ())),
                            preferred_element_type=F32) * ATTN_SCALE
        p = jnp.exp(s - jnp.max(s, axis=-1, keepdims=True))
        o = jnp.dot(p.astype(BF16), vm_ref[:, hs], preferred_element_type=F32)
        mem_heads.append((o / jnp.sum(p, axis=-1, keepdims=True)).astype(BF16))
    mem_out = jnp.concatenate(mem_heads, axis=-1)
    merged = gs_ref[:, 0:d].astype(F32) * jnp.dot(at_ref[...], pa_ref[...],
                                                  preferred_element_type=F32)
    merged += gs_ref[:, d:2 * d].astype(F32) * jnp.dot(lr_ref[...], pl_ref[...],
                                                       preferred_element_type=F32)
    merged += gs_ref[:, 2 * d:3 * d].astype(F32) * jnp.dot(mem_out, pm_ref[...],
                                                           preferred_element_type=F32)
    x1 = x_ref[...] + jnp.dot(merged.astype(BF16), wo_ref[...], preferred_element_type=F32)
    x1_ref[...] = x1
    h2_ref[...] = _rms(x1, gf_ref[...]).astype(BF16)


def _merge(x2d, attn, lru, qm, gs, km, vm, pa, pl_w, pm, wo, g_ffn, tm, seq, mem_len):
    n, d = x2d.shape
    per_seq = seq // tm
    row = lambda w: pl.BlockSpec((tm, w), lambda i: (i, 0))
    full = lambda a: pl.BlockSpec(a.shape, lambda i: (0,) * a.ndim)
    memspec = pl.BlockSpec((mem_len, MEM_Q_W), lambda i: (i // per_seq, 0))
    return pl.pallas_call(
        functools.partial(_merge_kernel, d=d),
        grid=(n // tm,),
        in_specs=[row(d), row(ATTN_Q_W), row(LRU_WIDTH), row(MEM_Q_W), row(N_BRANCH * d), memspec,
                  memspec, full(pa), full(pl_w), full(pm), full(wo), full(g_ffn)],
        out_specs=[row(d), row(d)],
        out_shape=[jax.ShapeDtypeStruct((n, d), F32), jax.ShapeDtypeStruct((n, d), BF16)],
        compiler_params=_params(),
        name="merge",
    )(x2d, attn, lru, qm, gs, km, vm, pa, pl_w, pm, wo, g_ffn)


def _allsub(x, op):
    for shift in (4, 2, 1):
        x = op(x, pltpu.roll(x, shift, 0))
    return x


def _tree(xs, op):
    xs = list(xs)
    while len(xs) > 1:
        nxt = [op(xs[i], xs[i + 1]) for i in range(0, len(xs) - 1, 2)]
        if len(xs) % 2:
            nxt.append(xs[-1])
        xs = nxt
    return xs[0]


def _extract_round(work, sub, k, rank=None):
    n_tiles = len(work)
    m = _allsub(_tree(work, jnp.maximum), jnp.maximum)
    chains = []
    for c0 in range(0, n_tiles, 4):
        acc = float(n_tiles)
        for r in reversed(range(c0, min(c0 + 4, n_tiles))):
            acc = jnp.where(work[r] == m, float(r), acc)
        chains.append(acc)
    first = _allsub(_tree(chains, jnp.minimum) * F32_ROWS + sub, jnp.minimum)
    first_tile = jnp.floor(first * (1.0 / F32_ROWS))
    knock = jnp.where(sub == first - first_tile * F32_ROWS, first_tile, -1.0)
    for r in range(n_tiles):
        hit = knock == float(r)
        work[r] = jnp.where(hit, -jnp.inf, work[r])
        if rank is not None:
            rank[r] = jnp.where(hit, float(k), rank[r])
    return m, knock


def _route_kernel(h2_ref, wq_ref, key_ref, cnt_ref, e1_ref, rk2_ref, e2_ref):
    assert PEER_TOPK == 16 and PEER_KEYS % F32_ROWS == 0
    n_heads = key_ref.shape[0]
    nt = (((1,), (1,)), ((), ()))

    def scores(hh):
        cols = slice(hh * 2 * PEER_HALF, (hh + 1) * 2 * PEER_HALF)
        q = jnp.dot(h2_ref[...], wq_ref[:, cols], preferred_element_type=F32).astype(BF16)
        return (lax.dot_general(key_ref[hh, 0], q[:, :PEER_HALF], nt, preferred_element_type=F32),
                lax.dot_general(key_ref[hh, 1], q[:, PEER_HALF:], nt, preferred_element_type=F32))

    nxt = scores(0)
    for hh in range(n_heads):
        s1, s2 = nxt
        if hh + 1 < n_heads:
            nxt = scores(hh + 1)
        _route_head(s1, s2, cnt_ref.at[hh], e1_ref.at[hh], rk2_ref.at[hh], e2_ref.at[hh])


def _route_head(s1, s2, cnt_ref, e1_ref, rk2_ref, e2_ref):
    t = s1.shape[1]
    n_tiles = PEER_KEYS // F32_ROWS
    tiles = lambda s: [s[r * F32_ROWS:(r + 1) * F32_ROWS] for r in range(n_tiles)]
    sub = lax.broadcasted_iota(jnp.int32, (F32_ROWS, t), 0).astype(F32)
    neg = -jnp.inf

    s1t, s2t = tiles(s1), tiles(s2)
    work1, work2 = list(s1t), list(s2t)
    rank2 = [jnp.full((F32_ROWS, t), float(PEER_TOPK), F32)] * n_tiles
    a, b, knock1 = [], [], []
    for k in range(PEER_TOPK):
        m, kn = _extract_round(work1, sub, k)
        a.append(m)
        knock1.append(kn)
        m, _ = _extract_round(work2, sub, k, rank2)
        b.append(m)

    def spread(vs):
        out = vs[0]
        for i in range(1, F32_ROWS):
            out = jnp.where(sub == float(i), vs[i], out)
        return out

    b_lo, b_hi, a_hi = spread(b[:F32_ROWS]), spread(b[F32_ROWS:]), spread(a[F32_ROWS:])
    b_r2, b_r4 = pltpu.roll(b_lo, 2, 0), pltpu.roll(b_lo, 4, 0)
    cand = [
        a[0] + b_lo,
        a[0] + b_hi,
        a[1] + b_lo,
        jnp.where(sub < 5.0, a[2] + b_lo, neg),
        jnp.where(sub < 4.0, a[3] + b_lo, jnp.where(sub < 7.0, a[4] + b_r4, neg)),
        jnp.where(sub < 2.0, a[5] + b_lo,
                  jnp.where(sub < 4.0, a[6] + b_r2, jnp.where(sub < 6.0, a[7] + b_r4, neg))),
        a_hi + b[0],
    ]
    work = list(cand)
    for k in range(PEER_TOPK):
        _extract_round(work, sub, k)
    sel = [jnp.where(w < c, 1.0, 0.0) for w, c in zip(work, cand)]
    v0 = a[0] + b[0]
    z = _allsub(_tree([s * jnp.exp(c - v0) for s, c in zip(sel, cand)], jnp.add), jnp.add)

    count = lambda x: _allsub(x, jnp.add)
    c = [count(sel[0] + sel[1]), count(sel[2]), count(sel[3]),
         count(jnp.where(sub < 4.0, sel[4], 0.0)), count(jnp.where(sub < 4.0, 0.0, sel[4])),
         count(jnp.where(sub < 2.0, sel[5], 0.0)),
         count(jnp.where(sub < 2.0, 0.0, jnp.where(sub < 4.0, sel[5], 0.0))),
         count(jnp.where(sub < 4.0, 0.0, sel[5]))]
    c += [count(jnp.where(sub == float(i), sel[6], 0.0)) for i in range(F32_ROWS)]

    inv_z = 1.0 / z
    for r in range(n_tiles):
        rows = slice(r * F32_ROWS, (r + 1) * F32_ROWS)
        cnt = jnp.zeros((F32_ROWS, t), F32)
        for p in range(PEER_TOPK):
            cnt = jnp.where(knock1[p] == float(r), c[p], cnt)
        cnt_ref[rows, :] = cnt
        e1_ref[rows, :] = jnp.where(work1[r] == neg, jnp.exp(s1t[r] - a[0]), 0.0)
    for r in range(0, n_tiles, 2):
        rows = slice(r * F32_ROWS, (r + 2) * F32_ROWS)
        rk = jnp.concatenate([rank2[r], rank2[r + 1]], axis=0)
        ex = jnp.concatenate([jnp.exp(s2t[r] - b[0]), jnp.exp(s2t[r + 1] - b[0])], axis=0)
        inv = jnp.concatenate([inv_z, inv_z], axis=0)
        rk2_ref[rows, :] = rk.astype(BF16)
        e2_ref[rows, :] = jnp.where(rk < PEER_TOPK, ex * inv, 0.0).astype(BF16)


def _route(h2, wq, keys, tf):
    n, d = h2.shape
    hp = ROUTE_HEADS_PER_STEP
    by_i = pl.BlockSpec((hp, PEER_KEYS, tf), lambda i, h: (h, 0, i))
    by_j = by_i
    by_i_shape = jax.ShapeDtypeStruct((PEER_HEADS, PEER_KEYS, n), F32)
    by_j_shape = jax.ShapeDtypeStruct((PEER_HEADS, PEER_KEYS, n), BF16)
    return pl.pallas_call(
        _route_kernel,
        grid=(n // tf, PEER_HEADS // hp),
        in_specs=[pl.BlockSpec((tf, d), lambda i, h: (i, 0)),
                  pl.BlockSpec((d, hp * 2 * PEER_HALF), lambda i, h: (0, h)),
                  pl.BlockSpec((hp, 2, PEER_KEYS, PEER_HALF), lambda i, h: (h, 0, 0, 0))],
        out_specs=[by_i, by_i, by_j, by_j],
        out_shape=[by_i_shape, by_i_shape, by_j_shape, by_j_shape],
        compiler_params=_params(2),
        name="route",
    )(h2, wq, keys)


def _peer_kernel(h2_ref, x1_ref, u_ref, vt_ref, cnt_ref, e1_ref, rk2_ref, e2_ref, gfin_ref,
                 o_ref, acc_ref, a_ref, w_ref, *, n_j, normalise):
    s = pl.program_id(0)
    sb = jnp.maximum(s - 1, 0)
    cur = s % 2
    prev = 1 - cur
    ne, t = a_ref.shape
    d = acc_ref.shape[0]
    n_i = ne // PEER_KEYS

    @pl.when(s == 0)
    def _():
        w_ref[...] = jnp.zeros_like(w_ref)

    @pl.when(sb % n_j == 0)
    def _():
        acc_ref[...] = jnp.zeros_like(acc_ref)

    def mm1(m, n):
        rows = slice(m * MM_ROWS, (m + 1) * MM_ROWS)
        cols = slice(n * MXU_DIM, (n + 1) * MXU_DIM)
        a_ref[rows, cols] = lax.dot_general(u_ref[rows, :], h2_ref[cols, :],
                                            (((1,), (1,)), ((), ())), preferred_element_type=F32)

    def mm2(m, n):
        rows = slice(m * MM_ROWS, (m + 1) * MM_ROWS)
        cols = slice(n * MXU_DIM, (n + 1) * MXU_DIM)
        acc_ref[rows, cols] += jnp.dot(vt_ref[rows, :], w_ref[prev, :, cols],
                                       preferred_element_type=F32)

    def gate_block(ii):
        g = None
        for h in range(PEER_HEADS):
            cnt = jnp.broadcast_to(cnt_ref[h, ii:ii + 1, :], (BF16_ROWS, t)).astype(BF16)
            e1 = jnp.broadcast_to(e1_ref[h, ii:ii + 1, :], (BF16_ROWS, t)).astype(BF16)
            cnt = pltpu.repeat(cnt, PEER_KEYS // BF16_ROWS, 0)
            e1 = pltpu.repeat(e1, PEER_KEYS // BF16_ROWS, 0)
            term = jnp.where(rk2_ref[h] < cnt, e2_ref[h], jnp.zeros((), BF16)) * e1
            g = term if g is None else g + term
        rows = slice(ii * PEER_KEYS, (ii + 1) * PEER_KEYS)
        w_ref[cur, rows, :] = g * _gelu_tanh(a_ref[rows, :].astype(BF16))

    n_cols = t // MXU_DIM
    ii_per_m = MM_ROWS // PEER_KEYS
    pieces2 = [(m, n) for m in range(d // MM_ROWS) for n in range(n_cols)]
    for n in range(n_cols):
        mm1(0, n)
    for ii in range(n_i):
        if ii % ii_per_m == 0:
            m_next = ii // ii_per_m + 1
            if m_next < ne // MM_ROWS:
                for n in range(n_cols):
                    mm1(m_next, n)
        if (ii * len(pieces2)) % n_i == 0:
            mm2(*pieces2[ii * len(pieces2) // n_i])
        gate_block(ii)

    @pl.when(jnp.logical_and(s >= 1, sb % n_j == n_j - 1))
    def _():
        x2 = x1_ref[...] + acc_ref[...].T
        o_ref[...] = _rms(x2, gfin_ref[...]) if normalise else x2


def _peer(h2, x1, u, vt, cnt, e1, rk2, e2, g_final, tp, ne, normalise):
    n, d = h2.shape
    n_exp = u.shape[0]
    n_i = ne // PEER_KEYS
    n_j = n_exp // ne
    n_steps = (n // tp) * n_j
    assert ne % MM_ROWS == 0 and d % MM_ROWS == 0 and tp % MXU_DIM == 0
    assert (n_i * MXU_DIM * MM_ROWS) % (d * tp) == 0
    sa = lambda s: jnp.minimum(s, n_steps - 1)
    sb = lambda s: jnp.maximum(s - 1, 0)
    tok_a = pl.BlockSpec((tp, d), lambda s: (sa(s) // n_j, 0))
    tok_b = pl.BlockSpec((tp, d), lambda s: (sb(s) // n_j, 0))
    by_i = pl.BlockSpec((PEER_HEADS, n_i, tp), lambda s: (0, sa(s) % n_j, sa(s) // n_j))
    by_j = pl.BlockSpec((PEER_HEADS, PEER_KEYS, tp), lambda s: (0, 0, sa(s) // n_j))
    return pl.pallas_call(
        functools.partial(_peer_kernel, n_j=n_j, normalise=normalise),
        grid=(n_steps + 1,),
        in_specs=[tok_a, tok_b,
                  pl.BlockSpec((ne, d), lambda s: (sa(s) % n_j, 0)),
                  pl.BlockSpec((d, ne), lambda s: (0, sb(s) % n_j)),
                  by_i, by_i, by_j, by_j,
                  pl.BlockSpec(g_final.shape, lambda s: (0, 0))],
        out_specs=tok_b,
        out_shape=jax.ShapeDtypeStruct((n, d), F32),
        scratch_shapes=[pltpu.VMEM((d, tp), F32), pltpu.VMEM((ne, tp), F32),
                        pltpu.VMEM((2, ne, tp), BF16)],
        compiler_params=_params(1),
        name="peer",
    )(h2, x1, u, vt, cnt, e1, rk2, e2, g_final)


def _tiles(batch, seq):
    n = batch * seq
    tm = min(256, seq)
    tf = min(512, n)
    tp = min(512, n)
    return tm, tf, tp


def kernel(x, mem, positions, g_mix, g_mem, w_in, gate_b, attn_sink, conv_w, conv_b, lru_wr, lru_br,
           lru_wi, lru_bi, lru_lambda, w_mem_kv, p_attn, p_lru, p_mem, w_out, g_ffn, w_peer_q,
           peer_sub_keys, peer_u, peer_v, g_final):
    batch, seq, d = x.shape
    mem_len = mem.shape[1]
    depth = w_in.shape[0]
    n = batch * seq
    n_exp = peer_u.shape[1]
    tm, tf, tp = _tiles(batch, seq)
    ne = min(2048, n_exp)

    inv_freq = ROPE_THETA ** (-jnp.arange(0, ROPE_DIM, 2, dtype=F32) / ROPE_DIM)
    invf = jnp.zeros((1, HEAD_DIM), F32).at[0, :ROPE_DIM].set(jnp.tile(inv_freq, 2))
    pos2d = positions.reshape(n, 1)
    offs = [int(o) for o in np.cumsum([0, ATTN_Q_W + 2 * ATTN_KV_W, 2 * LRU_WIDTH, MEM_Q_W])]
    mem2d = mem.reshape(batch * mem_len, d)

    xc = x.reshape(n, d)
    out = None
    for l in range(depth):
        w = w_in[l].astype(BF16)
        wqkv, wlru, wqm, wg = (w[:, offs[0]:offs[1]], w[:, offs[1]:offs[2]], w[:, offs[2]:offs[3]],
                               w[:, offs[3]:])
        q, k, v, xl, gg, qm, gs = _inproj(xc, pos2d, g_mix[l][None], invf, wqkv, wlru, wqm, wg,
                                          gate_b[l][None], tm)
        attn = _attn(attn_sink[l], q, k, v, batch, seq)
        lru = _lru(xl, gg, conv_w[l], conv_b[l][None], lru_wr[l].astype(BF16), lru_br[l],
                   lru_wi[l].astype(BF16), lru_bi[l], lru_lambda[l], batch, seq)
        km, vm = _memkv(mem2d, g_mem[l][None], w_mem_kv[l].astype(BF16), batch, mem_len)
        x1, h2 = _merge(xc, attn, lru, qm, gs, km, vm, p_attn[l].astype(BF16),
                        p_lru[l].astype(BF16), p_mem[l].astype(BF16), w_out[l].astype(BF16),
                        g_ffn[l][None], tm, seq, mem_len)
        cnt, e1, rk2, e2 = _route(h2, w_peer_q[l].astype(BF16), peer_sub_keys[l].astype(BF16), tf)
        last = l == depth - 1
        out = _peer(h2, x1, peer_u[l].astype(BF16), peer_v[l].astype(BF16).T, cnt, e1, rk2, e2,
                    g_final[None], tp, ne, normalise=last)
        xc = out
    return out.reshape(batch, seq, d)
```

```python
import functools
import math

import numpy as np
import jax
import jax.numpy as jnp
from jax import lax
from jax.experimental import pallas as pl
from jax.experimental.pallas import tpu as pltpu

F32 = jnp.float32
BF16 = jnp.bfloat16

HEAD_DIM = 128
ATTN_HEADS = 6
ATTN_KV_HEADS = 2
ATTN_GROUP = ATTN_HEADS // ATTN_KV_HEADS
WINDOW = 128
ROPE_DIM = HEAD_DIM // 4
ROPE_HALF = ROPE_DIM // 2
ROPE_THETA = 500000.0
LRU_WIDTH = 768
LRU_BLOCKS = 6
LRU_BLOCK_W = LRU_WIDTH // LRU_BLOCKS
CONV_WIDTH = 4
CONV_LEFT = CONV_WIDTH // 2
LRU_C = 8.0
MEM_HEADS = 4
N_BRANCH = 3
PEER_HEADS = 8
PEER_KEYS = 128
PEER_HALF = 128
PEER_TOPK = 16
EPS = 1e-6
NEG_INF = -1e30

ATTN_Q_W = ATTN_HEADS * HEAD_DIM
ATTN_KV_W = ATTN_KV_HEADS * HEAD_DIM
MEM_Q_W = MEM_HEADS * HEAD_DIM
ATTN_SCALE = HEAD_DIM ** -0.5

VMEM_LIMIT_BYTES = 56 * 1024 * 1024

F32_ROWS = 8
BF16_ROWS = 16
MXU_DIM = 256
MM_ROWS = 512
ROUTE_HEADS_PER_STEP = 4


def _params(n_axes=1):
    return pltpu.CompilerParams(
        dimension_semantics=("arbitrary",) * n_axes, vmem_limit_bytes=VMEM_LIMIT_BYTES)


def _rms(x, g):
    return x * lax.rsqrt(jnp.mean(x * x, axis=-1, keepdims=True) + EPS) * g


GELU_C0 = 2.0 * math.sqrt(2.0 / math.pi)
GELU_C1 = GELU_C0 * 0.044715


def _gelu_tanh(x):
    return x * jax.nn.sigmoid(x * (GELU_C0 + GELU_C1 * (x * x)))


def _inproj_kernel(x_ref, pos_ref, g_ref, invf_ref, wqkv_ref, wlru_ref, wqm_ref, wg_ref, gb_ref,
                   q_ref, k_ref, v_ref, xl_ref, gg_ref, qm_ref, gs_ref):
    hb = _rms(x_ref[...], g_ref[...]).astype(BF16)
    qkv = jnp.dot(hb, wqkv_ref[...], preferred_element_type=F32)
    ang = pos_ref[...].astype(F32) * invf_ref[...]
    cos = jnp.cos(ang)
    sin = jnp.sin(ang)
    lane = lax.broadcasted_iota(jnp.int32, (1, HEAD_DIM), 1)
    sin_lo = jnp.where(lane < ROPE_HALF, -sin, 0.0)
    sin_hi = jnp.where(lane >= ROPE_HALF, sin, 0.0)

    def rope(t):
        return (t * cos + pltpu.roll(t, HEAD_DIM - ROPE_HALF, 1) * sin_lo
                + pltpu.roll(t, ROPE_HALF, 1) * sin_hi)

    for j in range(ATTN_HEADS):
        sl = slice(j * HEAD_DIM, (j + 1) * HEAD_DIM)
        q_ref[:, sl] = rope(qkv[:, sl]).astype(BF16)
    for j in range(ATTN_KV_HEADS):
        sl = slice(j * HEAD_DIM, (j + 1) * HEAD_DIM)
        k_ref[:, sl] = rope(qkv[:, ATTN_Q_W + j * HEAD_DIM:ATTN_Q_W + (j + 1) * HEAD_DIM]).astype(BF16)
    v_ref[...] = qkv[:, ATTN_Q_W + ATTN_KV_W:].astype(BF16)

    lru = jnp.dot(hb, wlru_ref[...], preferred_element_type=F32)
    xl_ref[...] = lru[:, :LRU_WIDTH]
    gg_ref[...] = jax.nn.gelu(lru[:, LRU_WIDTH:]).astype(BF16)
    qm_ref[...] = jnp.dot(hb, wqm_ref[...], preferred_element_type=F32).astype(BF16)
    gates = jnp.dot(hb, wg_ref[...], preferred_element_type=F32) + gb_ref[...]
    gs_ref[...] = jax.nn.sigmoid(gates).astype(BF16)


def _inproj(x2d, pos2d, g_mix, invf, wqkv, wlru, wqm, wg, gate_b, tm):
    n, d = x2d.shape
    row = lambda w: pl.BlockSpec((tm, w), lambda i: (i, 0))
    full = lambda a: pl.BlockSpec(a.shape, lambda i: (0,) * a.ndim)
    widths = (ATTN_Q_W, ATTN_KV_W, ATTN_KV_W, LRU_WIDTH, LRU_WIDTH, MEM_Q_W, N_BRANCH * d)
    dtypes = (BF16, BF16, BF16, F32, BF16, BF16, BF16)
    return pl.pallas_call(
        _inproj_kernel,
        grid=(n // tm,),
        in_specs=[row(d), row(1), full(g_mix), full(invf), full(wqkv), full(wlru), full(wqm),
                  full(wg), full(gate_b)],
        out_specs=[row(w) for w in widths],
        out_shape=[jax.ShapeDtypeStruct((n, w), t) for w, t in zip(widths, dtypes)],
        compiler_params=_params(),
        name="inproj",
    )(x2d, pos2d, g_mix, invf, wqkv, wlru, wqm, wg, gate_b)


def _attn_kernel(sink_ref, q_ref, k_ref, v_ref, o_ref, *, seq):
    band = 3 * WINDOW
    masks = {}
    for n in range(seq // WINDOW):
        qs = n * WINDOW
        ks = min(max(qs - WINDOW, 0), seq - band)
        if ks - qs not in masks:
            dist = (ks - qs) + (lax.broadcasted_iota(jnp.int32, (WINDOW, band), 1)
                                - lax.broadcasted_iota(jnp.int32, (WINDOW, band), 0))
            masks[ks - qs] = jnp.where(dist <= WINDOW, dist, -WINDOW - 1) >= -WINDOW
        valid = masks[ks - qs]
        for kv in range(ATTN_KV_HEADS):
            kvs = slice(kv * HEAD_DIM, (kv + 1) * HEAD_DIM)
            kb = k_ref[ks:ks + band, kvs]
            vb = v_ref[ks:ks + band, kvs]
            heads = [kv * ATTN_GROUP + g for g in range(ATTN_GROUP)]
            q3 = jnp.concatenate(
                [q_ref[qs:qs + WINDOW, h * HEAD_DIM:(h + 1) * HEAD_DIM] for h in heads], axis=0)
            s3 = lax.dot_general(q3, kb, (((1,), (1,)), ((), ())),
                                 preferred_element_type=F32) * ATTN_SCALE
            ps, denoms = [], []
            for g, h in enumerate(heads):
                s = jnp.where(valid, s3[g * WINDOW:(g + 1) * WINDOW], NEG_INF)
                sink = sink_ref[h]
                m = jnp.maximum(jnp.max(s, axis=-1, keepdims=True), sink)
                p = jnp.exp(s - m)
                denoms.append(jnp.sum(p, axis=-1, keepdims=True) + jnp.exp(sink - m))
                ps.append(p.astype(BF16))
            o3 = jnp.dot(jnp.concatenate(ps, axis=0), vb, preferred_element_type=F32)
            for g, h in enumerate(heads):
                o = o3[g * WINDOW:(g + 1) * WINDOW] / denoms[g]
                o_ref[qs:qs + WINDOW, h * HEAD_DIM:(h + 1) * HEAD_DIM] = o.astype(BF16)


def _attn(sink, q, k, v, batch, seq):
    spec = lambda w: pl.BlockSpec((seq, w), lambda b: (b, 0))
    return pl.pallas_call(
        functools.partial(_attn_kernel, seq=seq),
        grid=(batch,),
        in_specs=[pl.BlockSpec(memory_space=pltpu.SMEM), spec(ATTN_Q_W), spec(ATTN_KV_W),
                  spec(ATTN_KV_W)],
        out_specs=spec(ATTN_Q_W),
        out_shape=jax.ShapeDtypeStruct((batch * seq, ATTN_Q_W), BF16),
        compiler_params=_params(),
        name="attn",
    )(sink, q, k, v)


def _chunk_carries(a_tot, h_tot, sub, reverse):
    n = F32_ROWS
    for s in (1, 2, 4):
        ok = sub < n - s if reverse else sub >= s
        shift = n - s if reverse else s
        h_tot = jnp.where(ok, a_tot * pltpu.roll(h_tot, shift, 0) + h_tot, h_tot)
        a_tot = jnp.where(ok, a_tot * pltpu.roll(a_tot, shift, 0), a_tot)
    if reverse:
        return jnp.where(sub < n - 1, pltpu.roll(h_tot, n - 1, 0), 0.0)
    return jnp.where(sub >= 1, pltpu.roll(h_tot, 1, 0), 0.0)


def _lru_kernel(xl_ref, gg_ref, cw_ref, cb_ref, wr_ref, br_ref, wi_ref, bi_ref, lam_ref, o_ref,
                a_ref, b_ref, at_ref, ht_ref, *, seq):
    chunk = seq // F32_ROWS
    t_idx = lax.broadcasted_iota(jnp.int32, (seq, LRU_BLOCK_W), 0)
    sub = lax.broadcasted_iota(jnp.int32, (F32_ROWS, LRU_BLOCK_W), 0)

    def step(j, carry):
        out = []
        for d in range(2):
            a_run, h_run = carry[2 * d], carry[2 * d + 1]
            jd = j if d == 0 else chunk - 1 - j
            a = a_ref[d, pl.ds(jd, F32_ROWS, stride=chunk), :]
            b = b_ref[d, pl.ds(jd, F32_ROWS, stride=chunk), :]
            h_run = a * h_run + b
            a_run = a * a_run
            rows = pl.ds(pl.multiple_of(jd * F32_ROWS, F32_ROWS), F32_ROWS)
            at_ref[d, rows, :] = a_run
            ht_ref[d, rows, :] = h_run
            out += [a_run, h_run]
        return tuple(out)

    for c in range(LRU_BLOCKS):
        cs = slice(c * LRU_BLOCK_W, (c + 1) * LRU_BLOCK_W)
        x = xl_ref[:, cs]
        xc = cb_ref[:, cs] + cw_ref[2:3, cs] * x
        xc = xc + cw_ref[0:1, cs] * jnp.where(t_idx >= 2, pltpu.roll(x, 2, 0), 0.0)
        xc = xc + cw_ref[1:2, cs] * jnp.where(t_idx >= 1, pltpu.roll(x, 1, 0), 0.0)
        xc = xc + cw_ref[3:4, cs] * jnp.where(t_idx < seq - 1, pltpu.roll(x, seq - 1, 0), 0.0)
        xcb = xc.astype(BF16)
        for d in range(2):
            r = jax.nn.sigmoid(jnp.dot(xcb, wr_ref[d, c], preferred_element_type=F32)
                               + br_ref[d:d + 1, cs])
            i = jax.nn.sigmoid(jnp.dot(xcb, wi_ref[d, c], preferred_element_type=F32)
                               + bi_ref[d:d + 1, cs])
            z = -lam_ref[d:d + 1, cs]
            softplus = jnp.maximum(z, 0.0) + jnp.log1p(jnp.exp(-jnp.abs(z)))
            log_a = (-LRU_C) * r * softplus
            a = jnp.exp(log_a)
            a_ref[d] = a
            b_ref[d] = jnp.sqrt(-jnp.tanh(log_a) * (a * a + 1.0)) * (i * xc)

        one = jnp.ones((F32_ROWS, LRU_BLOCK_W), F32)
        zero = jnp.zeros((F32_ROWS, LRU_BLOCK_W), F32)
        a_f, h_f, a_b, h_b = lax.fori_loop(0, chunk, step, (one, zero, one, zero), unroll=4)
        carry_f = _chunk_carries(a_f, h_f, sub, reverse=False)
        carry_b = _chunk_carries(a_b, h_b, sub, reverse=True)
        grouped = (chunk, F32_ROWS, LRU_BLOCK_W)
        h = (ht_ref[0].reshape(grouped) + at_ref[0].reshape(grouped) * carry_f[None]
             + ht_ref[1].reshape(grouped) + at_ref[1].reshape(grouped) * carry_b[None])
        ht_ref[0] = h.reshape(seq, LRU_BLOCK_W)
        for t0 in range(0, seq, BF16_ROWS):
            r, j0 = divmod(t0, chunk)
            rows = [ht_ref[0, pl.ds((j0 + k) * F32_ROWS + r, F32_ROWS, stride=F32_ROWS), :]
                    for k in range(0, BF16_ROWS, F32_ROWS)]
            ht = jnp.concatenate(rows, axis=0)
            o_ref[t0:t0 + BF16_ROWS, cs] = (
                ht * gg_ref[t0:t0 + BF16_ROWS, cs].astype(F32)).astype(BF16)


def _lru(xl, gg, conv_w, conv_b, wr, br, wi, bi, lam, batch, seq):
    spec = pl.BlockSpec((seq, LRU_WIDTH), lambda b: (b, 0))
    full = lambda a: pl.BlockSpec(a.shape, lambda b: (0,) * a.ndim)
    return pl.pallas_call(
        functools.partial(_lru_kernel, seq=seq),
        grid=(batch,),
        in_specs=[spec, spec, full(conv_w), full(conv_b), full(wr), full(br), full(wi), full(bi),
                  full(lam)],
        out_specs=spec,
        out_shape=jax.ShapeDtypeStruct((batch * seq, LRU_WIDTH), BF16),
        scratch_shapes=[pltpu.VMEM((2, seq, LRU_BLOCK_W), F32)] * 4,
        compiler_params=_params(),
        name="lru",
    )(xl, gg, conv_w, conv_b, wr, br, wi, bi, lam)


def _memkv_kernel(mem_ref, g_ref, w_ref, k_ref, v_ref):
    hb = _rms(mem_ref[...], g_ref[...]).astype(BF16)
    kv = jnp.dot(hb, w_ref[...], preferred_element_type=F32)
    k_ref[...] = kv[:, :MEM_Q_W].astype(BF16)
    v_ref[...] = kv[:, MEM_Q_W:].astype(BF16)


def _memkv(mem2d, g_mem, w, batch, mem_len):
    d = mem2d.shape[1]
    full = lambda a: pl.BlockSpec(a.shape, lambda b: (0,) * a.ndim)
    out = pl.BlockSpec((mem_len, MEM_Q_W), lambda b: (b, 0))
    return pl.pallas_call(
        _memkv_kernel,
        grid=(batch,),
        in_specs=[pl.BlockSpec((mem_len, d), lambda b: (b, 0)), full(g_mem), full(w)],
        out_specs=[out, out],
        out_shape=[jax.ShapeDtypeStruct((batch * mem_len, MEM_Q_W), BF16)] * 2,
        compiler_params=_params(),
        name="memkv",
    )(mem2d, g_mem, w)


def _merge_kernel(x_ref, at_ref, lr_ref, qm_ref, gs_ref, km_ref, vm_ref, pa_ref, pl_ref, pm_ref,
                  wo_ref, gf_ref, x1_ref, h2_ref, *, d):
    mem_heads = []
    for h in range(MEM_HEADS):
        hs = slice(h * HEAD_DIM, (h + 1) * HEAD_DIM)
        s = lax.dot_general(qm_ref[:, hs], km_ref[:, hs], (((1,), (1,)), ((), ())),
                            preferred_element_type=F32) * ATTN_SCALE
        p = jnp.exp(s - jnp.max(s, axis=-1, keepdims=True))
        o = jnp.dot(p.astype(BF16), vm_ref[:, hs], preferred_element_type=F32)
        mem_heads.append((o / jnp.sum(p, axis=-1, keepdims=True)).astype(BF16))
    mem_out = jnp.concatenate(mem_heads, axis=-1)
    merged = gs_ref[:, 0:d].astype(F32) * jnp.dot(at_ref[...], pa_ref[...],
                                                  preferred_element_type=F32)
    merged += gs_ref[:, d:2 * d].astype(F32) * jnp.dot(lr_ref[...], pl_ref[...],
                                                       preferred_element_type=F32)
    merged += gs_ref[:, 2 * d:3 * d].astype(F32) * jnp.dot(mem_out, pm_ref[...],
                                                           preferred_element_type=F32)
    x1 = x_ref[...] + jnp.dot(merged.astype(BF16), wo_ref[...], preferred_element_type=F32)
    x1_ref[...] = x1
    h2_ref[...] = _rms(x1, gf_ref[...]).astype(BF16)


def _merge(x2d, attn, lru, qm, gs, km, vm, pa, pl_w, pm, wo, g_ffn, tm, seq, mem_len):
    n, d = x2d.shape
    per_seq = seq // tm
    row = lambda w: pl.BlockSpec((tm, w), lambda i: (i, 0))
    full = lambda a: pl.BlockSpec(a.shape, lambda i: (0,) * a.ndim)
    memspec = pl.BlockSpec((mem_len, MEM_Q_W), lambda i: (i // per_seq, 0))
    return pl.pallas_call(
        functools.partial(_merge_kernel, d=d),
        grid=(n // tm,),
        in_specs=[row(d), row(ATTN_Q_W), row(LRU_WIDTH), row(MEM_Q_W), row(N_BRANCH * d), memspec,
                  memspec, full(pa), full(pl_w), full(pm), full(wo), full(g_ffn)],
        out_specs=[row(d), row(d)],
        out_shape=[jax.ShapeDtypeStruct((n, d), F32), jax.ShapeDtypeStruct((n, d), BF16)],
        compiler_params=_params(),
        name="merge",
    )(x2d, attn, lru, qm, gs, km, vm, pa, pl_w, pm, wo, g_ffn)


def _allsub(x, op):
    for shift in (4, 2, 1):
        x = op(x, pltpu.roll(x, shift, 0))
    return x


def _tree(xs, op):
    xs = list(xs)
    while len(xs) > 1:
        nxt = [op(xs[i], xs[i + 1]) for i in range(0, len(xs) - 1, 2)]
        if len(xs) % 2:
            nxt.append(xs[-1])
        xs = nxt
    return xs[0]


def _extract_round(work, sub, k, rank=None):
    n_tiles = len(work)
    m = _allsub(_tree(work, jnp.maximum), jnp.maximum)
    chains = []
    for c0 in range(0, n_tiles, 4):
        acc = float(n_tiles)
        for r in reversed(range(c0, min(c0 + 4, n_tiles))):
            acc = jnp.where(work[r] == m, float(r), acc)
        chains.append(acc)
    first = _allsub(_tree(chains, jnp.minimum) * F32_ROWS + sub, jnp.minimum)
    first_tile = jnp.floor(first * (1.0 / F32_ROWS))
    knock = jnp.where(sub == first - first_tile * F32_ROWS, first_tile, -1.0)
    for r in range(n_tiles):
        hit = knock == float(r)
        work[r] = jnp.where(hit, -jnp.inf, work[r])
        if rank is not None:
            rank[r] = jnp.where(hit, float(k), rank[r])
    return m, knock


def _route_kernel(h2_ref, wq_ref, key_ref, cnt_ref, e1_ref, rk2_ref, e2_ref):
    assert PEER_TOPK == 16 and PEER_KEYS % F32_ROWS == 0
    n_heads = key_ref.shape[0]
    nt = (((1,), (1,)), ((), ()))

    def scores(hh):
        cols = slice(hh * 2 * PEER_HALF, (hh + 1) * 2 * PEER_HALF)
        q = jnp.dot(h2_ref[...], wq_ref[:, cols], preferred_element_type=F32).astype(BF16)
        return (lax.dot_general(key_ref[hh, 0], q[:, :PEER_HALF], nt, preferred_element_type=F32),
                lax.dot_general(key_ref[hh, 1], q[:, PEER_HALF:], nt, preferred_element_type=F32))

    nxt = scores(0)
    for hh in range(n_heads):
        s1, s2 = nxt
        if hh + 1 < n_heads:
            nxt = scores(hh + 1)
        _route_head(s1, s2, cnt_ref.at[hh], e1_ref.at[hh], rk2_ref.at[hh], e2_ref.at[hh])


def _route_head(s1, s2, cnt_ref, e1_ref, rk2_ref, e2_ref):
    t = s1.shape[1]
    n_tiles = PEER_KEYS // F32_ROWS
    tiles = lambda s: [s[r * F32_ROWS:(r + 1) * F32_ROWS] for r in range(n_tiles)]
    sub = lax.broadcasted_iota(jnp.int32, (F32_ROWS, t), 0).astype(F32)
    neg = -jnp.inf

    s1t, s2t = tiles(s1), tiles(s2)
    work1, work2 = list(s1t), list(s2t)
    rank2 = [jnp.full((F32_ROWS, t), float(PEER_TOPK), F32)] * n_tiles
    a, b, knock1 = [], [], []
    for k in range(PEER_TOPK):
        m, kn = _extract_round(work1, sub, k)
        a.append(m)
        knock1.append(kn)
        m, _ = _extract_round(work2, sub, k, rank2)
        b.append(m)

    def spread(vs):
        out = vs[0]
        for i in range(1, F32_ROWS):
            out = jnp.where(sub == float(i), vs[i], out)
        return out

    b_lo, b_hi, a_hi = spread(b[:F32_ROWS]), spread(b[F32_ROWS:]), spread(a[F32_ROWS:])
    b_r2, b_r4 = pltpu.roll(b_lo, 2, 0), pltpu.roll(b_lo, 4, 0)
    cand = [
        a[0] + b_lo,
        a[0] + b_hi,
        a[1] + b_lo,
        jnp.where(sub < 5.0, a[2] + b_lo, neg),
        jnp.where(sub < 4.0, a[3] + b_lo, jnp.where(sub < 7.0, a[4] + b_r4, neg)),
        jnp.where(sub < 2.0, a[5] + b_lo,
                  jnp.where(sub < 4.0, a[6] + b_r2, jnp.where(sub < 6.0, a[7] + b_r4, neg))),
        a_hi + b[0],
    ]
    work = list(cand)
    for k in range(PEER_TOPK):
        _extract_round(work, sub, k)
    sel = [jnp.where(w < c, 1.0, 0.0) for w, c in zip(work, cand)]
    v0 = a[0] + b[0]
    z = _allsub(_tree([s * jnp.exp(c - v0) for s, c in zip(sel, cand)], jnp.add), jnp.add)

    count = lambda x: _allsub(x, jnp.add)
    c = [count(sel[0] + sel[1]), count(sel[2]), count(sel[3]),
         count(jnp.where(sub < 4.0, sel[4], 0.0)), count(jnp.where(sub < 4.0, 0.0, sel[4])),
         count(jnp.where(sub < 2.0, sel[5], 0.0)),
         count(jnp.where(sub < 2.0, 0.0, jnp.where(sub < 4.0, sel[5], 0.0))),
         count(jnp.where(sub < 4.0, 0.0, sel[5]))]
    c += [count(jnp.where(sub == float(i), sel[6], 0.0)) for i in range(F32_ROWS)]

    inv_z = 1.0 / z
    for r in range(n_tiles):
        rows = slice(r * F32_ROWS, (r + 1) * F32_ROWS)
        cnt = jnp.zeros((F32_ROWS, t), F32)
        for p in range(PEER_TOPK):
            cnt = jnp.where(knock1[p] == float(r), c[p], cnt)
        cnt_ref[rows, :] = cnt
        e1_ref[rows, :] = jnp.where(work1[r] == neg, jnp.exp(s1t[r] - a[0]), 0.0)
    for r in range(0, n_tiles, 2):
        rows = slice(r * F32_ROWS, (r + 2) * F32_ROWS)
        rk = jnp.concatenate([rank2[r], rank2[r + 1]], axis=0)
        ex = jnp.concatenate([jnp.exp(s2t[r] - b[0]), jnp.exp(s2t[r + 1] - b[0])], axis=0)
        inv = jnp.concatenate([inv_z, inv_z], axis=0)
        rk2_ref[rows, :] = rk.astype(BF16)
        e2_ref[rows, :] = jnp.where(rk < PEER_TOPK, ex * inv, 0.0).astype(BF16)


def _route(h2, wq, keys, tf):
    n, d = h2.shape
    hp = ROUTE_HEADS_PER_STEP
    by_i = pl.BlockSpec((hp, PEER_KEYS, tf), lambda i, h: (h, 0, i))
    by_j = by_i
    by_i_shape = jax.ShapeDtypeStruct((PEER_HEADS, PEER_KEYS, n), F32)
    by_j_shape = jax.ShapeDtypeStruct((PEER_HEADS, PEER_KEYS, n), BF16)
    return pl.pallas_call(
        _route_kernel,
        grid=(n // tf, PEER_HEADS // hp),
        in_specs=[pl.BlockSpec((tf, d), lambda i, h: (i, 0)),
                  pl.BlockSpec((d, hp * 2 * PEER_HALF), lambda i, h: (0, h)),
                  pl.BlockSpec((hp, 2, PEER_KEYS, PEER_HALF), lambda i, h: (h, 0, 0, 0))],
        out_specs=[by_i, by_i, by_j, by_j],
        out_shape=[by_i_shape, by_i_shape, by_j_shape, by_j_shape],
        compiler_params=_params(2),
        name="route",
    )(h2, wq, keys)


def _peer_kernel(h2_ref, x1_ref, u_ref, vt_ref, cnt_ref, e1_ref, rk2_ref, e2_ref, gfin_ref,
                 o_ref, acc_ref, a_ref, w_ref, *, n_j, normalise):
    s = pl.program_id(0)
    sb = jnp.maximum(s - 1, 0)
    cur = s % 2
    prev = 1 - cur
    ne, t = a_ref.shape
    d = acc_ref.shape[0]
    n_i = ne // PEER_KEYS

    @pl.when(s == 0)
    def _():
        w_ref[...] = jnp.zeros_like(w_ref)

    @pl.when(sb % n_j == 0)
    def _():
        acc_ref[...] = jnp.zeros_like(acc_ref)

    def mm1(m, n):
        rows = slice(m * MM_ROWS, (m + 1) * MM_ROWS)
        cols = slice(n * MXU_DIM, (n + 1) * MXU_DIM)
        a_ref[rows, cols] = lax.dot_general(u_ref[rows, :], h2_ref[cols, :],
                                            (((1,), (1,)), ((), ())), preferred_element_type=F32)

    def mm2(m, n):
        rows = slice(m * MM_ROWS, (m + 1) * MM_ROWS)
        cols = slice(n * MXU_DIM, (n + 1) * MXU_DIM)
        acc_ref[rows, cols] += jnp.dot(vt_ref[rows, :], w_ref[prev, :, cols],
                                       preferred_element_type=F32)

    def gate_block(ii):
        g = None
        for h in range(PEER_HEADS):
            cnt = jnp.broadcast_to(cnt_ref[h, ii:ii + 1, :], (BF16_ROWS, t)).astype(BF16)
            e1 = jnp.broadcast_to(e1_ref[h, ii:ii + 1, :], (BF16_ROWS, t)).astype(BF16)
            cnt = pltpu.repeat(cnt, PEER_KEYS // BF16_ROWS, 0)
            e1 = pltpu.repeat(e1, PEER_KEYS // BF16_ROWS, 0)
            term = jnp.where(rk2_ref[h] < cnt, e2_ref[h], jnp.zeros((), BF16)) * e1
            g = term if g is None else g + term
        rows = slice(ii * PEER_KEYS, (ii + 1) * PEER_KEYS)
        w_ref[cur, rows, :] = g * _gelu_tanh(a_ref[rows, :].astype(BF16))

    n_cols = t // MXU_DIM
    ii_per_m = MM_ROWS // PEER_KEYS
    pieces2 = [(m, n) for m in range(d // MM_ROWS) for n in range(n_cols)]
    for n in range(n_cols):
        mm1(0, n)
    for ii in range(n_i):
        if ii % ii_per_m == 0:
            m_next = ii // ii_per_m + 1
            if m_next < ne // MM_ROWS:
                for n in range(n_cols):
                    mm1(m_next, n)
        if (ii * len(pieces2)) % n_i == 0:
            mm2(*pieces2[ii * len(pieces2) // n_i])
        gate_block(ii)

    @pl.when(jnp.logical_and(s >= 1, sb % n_j == n_j - 1))
    def _():
        x2 = x1_ref[...] + acc_ref[...].T
        o_ref[...] = _rms(x2, gfin_ref[...]) if normalise else x2


def _peer(h2, x1, u, vt, cnt, e1, rk2, e2, g_final, tp, ne, normalise):
    n, d = h2.shape
    n_exp = u.shape[0]
    n_i = ne // PEER_KEYS
    n_j = n_exp // ne
    n_steps = (n // tp) * n_j
    assert ne % MM_ROWS == 0 and d % MM_ROWS == 0 and tp % MXU_DIM == 0
    assert (n_i * MXU_DIM * MM_ROWS) % (d * tp) == 0
    sa = lambda s: jnp.minimum(s, n_steps - 1)
    sb = lambda s: jnp.maximum(s - 1, 0)
    tok_a = pl.BlockSpec((tp, d), lambda s: (sa(s) // n_j, 0))
    tok_b = pl.BlockSpec((tp, d), lambda s: (sb(s) // n_j, 0))
    by_i = pl.BlockSpec((PEER_HEADS, n_i, tp), lambda s: (0, sa(s) % n_j, sa(s) // n_j))
    by_j = pl.BlockSpec((PEER_HEADS, PEER_KEYS, tp), lambda s: (0, 0, sa(s) // n_j))
    return pl.pallas_call(
        functools.partial(_peer_kernel, n_j=n_j, normalise=normalise),
        grid=(n_steps + 1,),
        in_specs=[tok_a, tok_b,
                  pl.BlockSpec((ne, d), lambda s: (sa(s) % n_j, 0)),
                  pl.BlockSpec((d, ne), lambda s: (0, sb(s) % n_j)),
                  by_i, by_i, by_j, by_j,
                  pl.BlockSpec(g_final.shape, lambda s: (0, 0))],
        out_specs=tok_b,
        out_shape=jax.ShapeDtypeStruct((n, d), F32),
        scratch_shapes=[pltpu.VMEM((d, tp), F32), pltpu.VMEM((ne, tp), F32),
                        pltpu.VMEM((2, ne, tp), BF16)],
        compiler_params=_params(1),
        name="peer",
    )(h2, x1, u, vt, cnt, e1, rk2, e2, g_final)


def _tiles(batch, seq):
    n = batch * seq
    tm = min(256, seq)
    tf = min(512, n)
    tp = min(512, n)
    return tm, tf, tp


def kernel(x, mem, positions, g_mix, g_mem, w_in, gate_b, attn_sink, conv_w, conv_b, lru_wr, lru_br,
           lru_wi, lru_bi, lru_lambda, w_mem_kv, p_attn, p_lru, p_mem, w_out, g_ffn, w_peer_q,
           peer_sub_keys, peer_u, peer_v, g_final):
    batch, seq, d = x.shape
    mem_len = mem.shape[1]
    depth = w_in.shape[0]
    n = batch * seq
    n_exp = peer_u.shape[1]
    tm, tf, tp = _tiles(batch, seq)
    ne = min(2048, n_exp)

    inv_freq = ROPE_THETA ** (-jnp.arange(0, ROPE_DIM, 2, dtype=F32) / ROPE_DIM)
    invf = jnp.zeros((1, HEAD_DIM), F32).at[0, :ROPE_DIM].set(jnp.tile(inv_freq, 2))
    pos2d = positions.reshape(n, 1)
    offs = [int(o) for o in np.cumsum([0, ATTN_Q_W + 2 * ATTN_KV_W, 2 * LRU_WIDTH, MEM_Q_W])]
    mem2d = mem.reshape(batch * mem_len, d)

    xc = x.reshape(n, d)
    out = None
    for l in range(depth):
        w = w_in[l].astype(BF16)
        wqkv, wlru, wqm, wg = (w[:, offs[0]:offs[1]], w[:, offs[1]:offs[2]], w[:, offs[2]:offs[3]],
                               w[:, offs[3]:])
        q, k, v, xl, gg, qm, gs = _inproj(xc, pos2d, g_mix[l][None], invf, wqkv, wlru, wqm, wg,
                                          gate_b[l][None], tm)
        attn = _attn(attn_sink[l], q, k, v, batch, seq)
        lru = _lru(xl, gg, conv_w[l], conv_b[l][None], lru_wr[l].astype(BF16), lru_br[l],
                   lru_wi[l].astype(BF16), lru_bi[l], lru_lambda[l], batch, seq)
        km, vm = _memkv(mem2d, g_mem[l][None], w_mem_kv[l].astype(BF16), batch, mem_len)
        x1, h2 = _merge(xc, attn, lru, qm, gs, km, vm, p_attn[l].astype(BF16),
                        p_lru[l].astype(BF16), p_mem[l].astype(BF16), w_out[l].astype(BF16),
                        g_ffn[l][None], tm, seq, mem_len)
        cnt, e1, rk2, e2 = _route(h2, w_peer_q[l].astype(BF16), peer_sub_keys[l].astype(BF16), tf)
        last = l == depth - 1
        out = _peer(h2, x1, peer_u[l].astype(BF16), peer_v[l].astype(BF16).T, cnt, e1, rk2, e2,
                    g_final[None], tp, ne, normalise=last)
        xc = out
    return out.reshape(batch, seq, d)
```

```python
import functools
import math

import numpy as np
import jax
import jax.numpy as jnp
from jax import lax
from jax.experimental import pallas as pl
from jax.experimental.pallas import tpu as pltpu

F32 = jnp.float32
BF16 = jnp.bfloat16

HEAD_DIM = 128
ATTN_HEADS = 6
ATTN_KV_HEADS = 2
ATTN_GROUP = ATTN_HEADS // ATTN_KV_HEADS
WINDOW = 128
ROPE_DIM = HEAD_DIM // 4
ROPE_HALF = ROPE_DIM // 2
ROPE_THETA = 500000.0
LRU_WIDTH = 768
LRU_BLOCKS = 6
LRU_BLOCK_W = LRU_WIDTH // LRU_BLOCKS
CONV_WIDTH = 4
CONV_LEFT = CONV_WIDTH // 2
LRU_C = 8.0
MEM_HEADS = 4
N_BRANCH = 3
PEER_HEADS = 8
PEER_KEYS = 128
PEER_HALF = 128
PEER_TOPK = 16
EPS = 1e-6
NEG_INF = -1e30

ATTN_Q_W = ATTN_HEADS * HEAD_DIM
ATTN_KV_W = ATTN_KV_HEADS * HEAD_DIM
MEM_Q_W = MEM_HEADS * HEAD_DIM
ATTN_SCALE = HEAD_DIM ** -0.5

VMEM_LIMIT_BYTES = 56 * 1024 * 1024

F32_ROWS = 8
BF16_ROWS = 16
LANES = 128
MXU_DIM = 256
MM_ROWS = 512
ROUTE_HEADS_PER_STEP = 4


def _params(n_axes=1):
    return pltpu.CompilerParams(
        dimension_semantics=("arbitrary",) * n_axes, vmem_limit_bytes=VMEM_LIMIT_BYTES)


def _rms(x, g):
    return x * lax.rsqrt(jnp.mean(x * x, axis=-1, keepdims=True) + EPS) * g


GELU_C0 = math.sqrt(2.0 / math.pi)
GELU_C1 = GELU_C0 * 0.044715


def _gelu_tanh(x):
    half = 0.5 * x
    return half * jnp.tanh(x * (GELU_C0 + GELU_C1 * (x * x))) + half


def _inproj_kernel(x_ref, pos_ref, g_ref, invf_ref, wqkv_ref, wlru_ref, wqm_ref, wg_ref, gb_ref,
                   q_ref, k_ref, v_ref, xl_ref, gg_ref, qm_ref, gs_ref):
    hb = _rms(x_ref[...], g_ref[...]).astype(BF16)
    qkv = jnp.dot(hb, wqkv_ref[...], preferred_element_type=F32)
    ang = pos_ref[...].astype(F32) * invf_ref[...]
    cos = jnp.cos(ang)
    sin = jnp.sin(ang)
    lane = lax.broadcasted_iota(jnp.int32, (1, HEAD_DIM), 1)
    sin_lo = jnp.where(lane < ROPE_HALF, -sin, 0.0)
    sin_hi = jnp.where(lane >= ROPE_HALF, sin, 0.0)

    def rope(t):
        return (t * cos + pltpu.roll(t, HEAD_DIM - ROPE_HALF, 1) * sin_lo
                + pltpu.roll(t, ROPE_HALF, 1) * sin_hi)

    for j in range(ATTN_HEADS):
        sl = slice(j * HEAD_DIM, (j + 1) * HEAD_DIM)
        q_ref[:, sl] = rope(qkv[:, sl]).astype(BF16)
    for j in range(ATTN_KV_HEADS):
        sl = slice(j * HEAD_DIM, (j + 1) * HEAD_DIM)
        k_ref[:, sl] = rope(qkv[:, ATTN_Q_W + j * HEAD_DIM:ATTN_Q_W + (j + 1) * HEAD_DIM]).astype(BF16)
    v_ref[...] = qkv[:, ATTN_Q_W + ATTN_KV_W:].astype(BF16)

    lru = jnp.dot(hb, wlru_ref[...], preferred_element_type=F32)
    xl_ref[...] = lru[:, :LRU_WIDTH]
    gg_ref[...] = jax.nn.gelu(lru[:, LRU_WIDTH:]).astype(BF16)
    qm_ref[...] = jnp.dot(hb, wqm_ref[...], preferred_element_type=F32).astype(BF16)
    gates = jnp.dot(hb, wg_ref[...], preferred_element_type=F32) + gb_ref[...]
    gs_ref[...] = jax.nn.sigmoid(gates).astype(BF16)


def _inproj(x2d, pos2d, g_mix, invf, wqkv, wlru, wqm, wg, gate_b, tm):
    n, d = x2d.shape
    row = lambda w: pl.BlockSpec((tm, w), lambda i: (i, 0))
    full = lambda a: pl.BlockSpec(a.shape, lambda i: (0,) * a.ndim)
    widths = (ATTN_Q_W, ATTN_KV_W, ATTN_KV_W, LRU_WIDTH, LRU_WIDTH, MEM_Q_W, N_BRANCH * d)
    dtypes = (BF16, BF16, BF16, F32, BF16, BF16, BF16)
    return pl.pallas_call(
        _inproj_kernel,
        grid=(n // tm,),
        in_specs=[row(d), row(1), full(g_mix), full(invf), full(wqkv), full(wlru), full(wqm),
                  full(wg), full(gate_b)],
        out_specs=[row(w) for w in widths],
        out_shape=[jax.ShapeDtypeStruct((n, w), t) for w, t in zip(widths, dtypes)],
        compiler_params=_params(),
        name="inproj",
    )(x2d, pos2d, g_mix, invf, wqkv, wlru, wqm, wg, gate_b)


def _attn_kernel(sink_ref, q_ref, k_ref, v_ref, o_ref, *, seq):
    band = 3 * WINDOW
    masks = {}
    for n in range(seq // WINDOW):
        qs = n * WINDOW
        ks = min(max(qs - WINDOW, 0), seq - band)
        if ks - qs not in masks:
            dist = (ks - qs) + (lax.broadcasted_iota(jnp.int32, (WINDOW, band), 1)
                                - lax.broadcasted_iota(jnp.int32, (WINDOW, band), 0))
            masks[ks - qs] = jnp.where(dist <= WINDOW, dist, -WINDOW - 1) >= -WINDOW
        valid = masks[ks - qs]
        for kv in range(ATTN_KV_HEADS):
            kvs = slice(kv * HEAD_DIM, (kv + 1) * HEAD_DIM)
            kb = k_ref[ks:ks + band, kvs]
            vb = v_ref[ks:ks + band, kvs]
            heads = [kv * ATTN_GROUP + g for g in range(ATTN_GROUP)]
            q3 = jnp.concatenate(
                [q_ref[qs:qs + WINDOW, h * HEAD_DIM:(h + 1) * HEAD_DIM] for h in heads], axis=0)
            s3 = lax.dot_general(q3, kb, (((1,), (1,)), ((), ())),
                                 preferred_element_type=F32) * ATTN_SCALE
            ps, denoms = [], []
            for g, h in enumerate(heads):
                s = jnp.where(valid, s3[g * WINDOW:(g + 1) * WINDOW], NEG_INF)
                sink = sink_ref[h]
                m = jnp.maximum(jnp.max(s, axis=-1, keepdims=True), sink)
                p = jnp.exp(s - m)
                denoms.append(jnp.sum(p, axis=-1, keepdims=True) + jnp.exp(sink - m))
                ps.append(p.astype(BF16))
            o3 = jnp.dot(jnp.concatenate(ps, axis=0), vb, preferred_element_type=F32)
            for g, h in enumerate(heads):
                o = o3[g * WINDOW:(g + 1) * WINDOW] / denoms[g]
                o_ref[qs:qs + WINDOW, h * HEAD_DIM:(h + 1) * HEAD_DIM] = o.astype(BF16)


def _attn(sink, q, k, v, batch, seq):
    spec = lambda w: pl.BlockSpec((seq, w), lambda b: (b, 0))
    return pl.pallas_call(
        functools.partial(_attn_kernel, seq=seq),
        grid=(batch,),
        in_specs=[pl.BlockSpec(memory_space=pltpu.SMEM), spec(ATTN_Q_W), spec(ATTN_KV_W),
                  spec(ATTN_KV_W)],
        out_specs=spec(ATTN_Q_W),
        out_shape=jax.ShapeDtypeStruct((batch * seq, ATTN_Q_W), BF16),
        compiler_params=_params(),
        name="attn",
    )(sink, q, k, v)


def _chunk_carries(a_tot, h_tot, sub, reverse):
    n = F32_ROWS
    for s in (1, 2, 4):
        ok = sub < n - s if reverse else sub >= s
        shift = n - s if reverse else s
        h_tot = jnp.where(ok, a_tot * pltpu.roll(h_tot, shift, 0) + h_tot, h_tot)
        a_tot = jnp.where(ok, a_tot * pltpu.roll(a_tot, shift, 0), a_tot)
    if reverse:
        return jnp.where(sub < n - 1, pltpu.roll(h_tot, n - 1, 0), 0.0)
    return jnp.where(sub >= 1, pltpu.roll(h_tot, 1, 0), 0.0)


def _lru_kernel(xl_ref, gg_ref, cw_ref, cb_ref, wr_ref, br_ref, wi_ref, bi_ref, lam_ref, o_ref,
                a_ref, b_ref, at_ref, ht_ref, *, seq):
    chunk = seq // F32_ROWS
    t_idx = lax.broadcasted_iota(jnp.int32, (seq, LRU_BLOCK_W), 0)
    sub = lax.broadcasted_iota(jnp.int32, (F32_ROWS, LRU_BLOCK_W), 0)

    def step(j, carry):
        out = []
        for d in range(2):
            a_run, h_run = carry[2 * d], carry[2 * d + 1]
            jd = j if d == 0 else chunk - 1 - j
            a = a_ref[d, pl.ds(jd, F32_ROWS, stride=chunk), :]
            b = b_ref[d, pl.ds(jd, F32_ROWS, stride=chunk), :]
            h_run = a * h_run + b
            a_run = a * a_run
            rows = pl.ds(pl.multiple_of(jd * F32_ROWS, F32_ROWS), F32_ROWS)
            at_ref[d, rows, :] = a_run
            ht_ref[d, rows, :] = h_run
            out += [a_run, h_run]
        return tuple(out)

    for c in range(LRU_BLOCKS):
        cs = slice(c * LRU_BLOCK_W, (c + 1) * LRU_BLOCK_W)
        x = xl_ref[:, cs]
        xc = cb_ref[:, cs] + cw_ref[2:3, cs] * x
        xc = xc + cw_ref[0:1, cs] * jnp.where(t_idx >= 2, pltpu.roll(x, 2, 0), 0.0)
        xc = xc + cw_ref[1:2, cs] * jnp.where(t_idx >= 1, pltpu.roll(x, 1, 0), 0.0)
        xc = xc + cw_ref[3:4, cs] * jnp.where(t_idx < seq - 1, pltpu.roll(x, seq - 1, 0), 0.0)
        xcb = xc.astype(BF16)
        for d in range(2):
            r = jax.nn.sigmoid(jnp.dot(xcb, wr_ref[d, c], preferred_element_type=F32)
                               + br_ref[d:d + 1, cs])
            i = jax.nn.sigmoid(jnp.dot(xcb, wi_ref[d, c], preferred_element_type=F32)
                               + bi_ref[d:d + 1, cs])
            z = -lam_ref[d:d + 1, cs]
            softplus = jnp.maximum(z, 0.0) + jnp.log1p(jnp.exp(-jnp.abs(z)))
            log_a = (-LRU_C) * r * softplus
            a = jnp.exp(log_a)
            a_ref[d] = a
            b_ref[d] = jnp.sqrt(-jnp.tanh(log_a) * (a * a + 1.0)) * (i * xc)

        one = jnp.ones((F32_ROWS, LRU_BLOCK_W), F32)
        zero = jnp.zeros((F32_ROWS, LRU_BLOCK_W), F32)
        a_f, h_f, a_b, h_b = lax.fori_loop(0, chunk, step, (one, zero, one, zero), unroll=4)
        carry_f = _chunk_carries(a_f, h_f, sub, reverse=False)
        carry_b = _chunk_carries(a_b, h_b, sub, reverse=True)
        grouped = (chunk, F32_ROWS, LRU_BLOCK_W)
        h = (ht_ref[0].reshape(grouped) + at_ref[0].reshape(grouped) * carry_f[None]
             + ht_ref[1].reshape(grouped) + at_ref[1].reshape(grouped) * carry_b[None])
        ht_ref[0] = h.reshape(seq, LRU_BLOCK_W)
        for t0 in range(0, seq, BF16_ROWS):
            r, j0 = divmod(t0, chunk)
            rows = [ht_ref[0, pl.ds((j0 + k) * F32_ROWS + r, F32_ROWS, stride=F32_ROWS), :]
                    for k in range(0, BF16_ROWS, F32_ROWS)]
            ht = jnp.concatenate(rows, axis=0)
            o_ref[t0:t0 + BF16_ROWS, cs] = (
                ht * gg_ref[t0:t0 + BF16_ROWS, cs].astype(F32)).astype(BF16)


def _lru(xl, gg, conv_w, conv_b, wr, br, wi, bi, lam, batch, seq):
    spec = pl.BlockSpec((seq, LRU_WIDTH), lambda b: (b, 0))
    full = lambda a: pl.BlockSpec(a.shape, lambda b: (0,) * a.ndim)
    return pl.pallas_call(
        functools.partial(_lru_kernel, seq=seq),
        grid=(batch,),
        in_specs=[spec, spec, full(conv_w), full(conv_b), full(wr), full(br), full(wi), full(bi),
                  full(lam)],
        out_specs=spec,
        out_shape=jax.ShapeDtypeStruct((batch * seq, LRU_WIDTH), BF16),
        scratch_shapes=[pltpu.VMEM((2, seq, LRU_BLOCK_W), F32)] * 4,
        compiler_params=_params(),
        name="lru",
    )(xl, gg, conv_w, conv_b, wr, br, wi, bi, lam)


def _memkv_kernel(mem_ref, g_ref, w_ref, k_ref, v_ref):
    hb = _rms(mem_ref[...], g_ref[...]).astype(BF16)
    kv = jnp.dot(hb, w_ref[...], preferred_element_type=F32)
    k_ref[...] = kv[:, :MEM_Q_W].astype(BF16)
    v_ref[...] = kv[:, MEM_Q_W:].astype(BF16)


def _memkv(mem2d, g_mem, w, batch, mem_len):
    d = mem2d.shape[1]
    full = lambda a: pl.BlockSpec(a.shape, lambda b: (0,) * a.ndim)
    out = pl.BlockSpec((mem_len, MEM_Q_W), lambda b: (b, 0))
    return pl.pallas_call(
        _memkv_kernel,
        grid=(batch,),
        in_specs=[pl.BlockSpec((mem_len, d), lambda b: (b, 0)), full(g_mem), full(w)],
        out_specs=[out, out],
        out_shape=[jax.ShapeDtypeStruct((batch * mem_len, MEM_Q_W), BF16)] * 2,
        compiler_params=_params(),
        name="memkv",
    )(mem2d, g_mem, w)


def _merge_kernel(x_ref, at_ref, lr_ref, qm_ref, gs_ref, km_ref, vm_ref, pa_ref, pl_ref, pm_ref,
                  wo_ref, gf_ref, x1_ref, h2_ref, *, d):
    mem_heads = []
    for h in range(MEM_HEADS):
        hs = slice(h * HEAD_DIM, (h + 1) * HEAD_DIM)
        s = lax.dot_general(qm_ref[:, hs], km_ref[:, hs], (((1,), (1,)), ((), ())),
                            preferred_element_type=F32) * ATTN_SCALE
        p = jnp.exp(s - jnp.max(s, axis=-1, keepdims=True))
        o = jnp.dot(p.astype(BF16), vm_ref[:, hs], preferred_element_type=F32)
        mem_heads.append((o / jnp.sum(p, axis=-1, keepdims=True)).astype(BF16))
    mem_out = jnp.concatenate(mem_heads, axis=-1)
    merged = gs_ref[:, 0:d].astype(F32) * jnp.dot(at_ref[...], pa_ref[...],
                                                  preferred_element_type=F32)
    merged += gs_ref[:, d:2 * d].astype(F32) * jnp.dot(lr_ref[...], pl_ref[...],
                                                       preferred_element_type=F32)
    merged += gs_ref[:, 2 * d:3 * d].astype(F32) * jnp.dot(mem_out, pm_ref[...],
                                                           preferred_element_type=F32)
    x1 = x_ref[...] + jnp.dot(merged.astype(BF16), wo_ref[...], preferred_element_type=F32)
    x1_ref[...] = x1
    h2_ref[...] = _rms(x1, gf_ref[...]).astype(BF16)


def _merge(x2d, attn, lru, qm, gs, km, vm, pa, pl_w, pm, wo, g_ffn, tm, seq, mem_len):
    n, d = x2d.shape
    per_seq = seq // tm
    row = lambda w: pl.BlockSpec((tm, w), lambda i: (i, 0))
    full = lambda a: pl.BlockSpec(a.shape, lambda i: (0,) * a.ndim)
    memspec = pl.BlockSpec((mem_len, MEM_Q_W), lambda i: (i // per_seq, 0))
    return pl.pallas_call(
        functools.partial(_merge_kernel, d=d),
        grid=(n // tm,),
        in_specs=[row(d), row(ATTN_Q_W), row(LRU_WIDTH), row(MEM_Q_W), row(N_BRANCH * d), memspec,
                  memspec, full(pa), full(pl_w), full(pm), full(wo), full(g_ffn)],
        out_specs=[row(d), row(d)],
        out_shape=[jax.ShapeDtypeStruct((n, d), F32), jax.ShapeDtypeStruct((n, d), BF16)],
        compiler_params=_params(),
        name="merge",
    )(x2d, attn, lru, qm, gs, km, vm, pa, pl_w, pm, wo, g_ffn)


def _allsub(x, op):
    for shift in (4, 2, 1):
        x = op(x, pltpu.roll(x, shift, 0))
    return x


def _tree(xs, op):
    xs = list(xs)
    while len(xs) > 1:
        nxt = [op(xs[i], xs[i + 1]) for i in range(0, len(xs) - 1, 2)]
        if len(xs) % 2:
            nxt.append(xs[-1])
        xs = nxt
    return xs[0]


def _extract_round(work, sub, k, rank=None):
    n_tiles = len(work)
    m = _allsub(_tree(work, jnp.maximum), jnp.maximum)
    chains = []
    for c0 in range(0, n_tiles, 4):
        acc = float(n_tiles)
        for r in reversed(range(c0, min(c0 + 4, n_tiles))):
            acc = jnp.where(work[r] == m, float(r), acc)
        chains.append(acc)
    first = _allsub(_tree(chains, jnp.minimum) * F32_ROWS + sub, jnp.minimum)
    first_tile = jnp.floor(first * (1.0 / F32_ROWS))
    knock = jnp.where(sub == first - first_tile * F32_ROWS, first_tile, -1.0)
    for r in range(n_tiles):
        hit = knock == float(r)
        work[r] = jnp.where(hit, -jnp.inf, work[r])
        if rank is not None:
            rank[r] = jnp.where(hit, float(k), rank[r])
    return m, knock


def _route_kernel(h2_ref, wq_ref, key_ref, cnt_ref, e1_ref, rk2_ref, e2_ref):
    assert PEER_TOPK == 16 and PEER_KEYS % F32_ROWS == 0
    n_heads = key_ref.shape[0]
    nt = (((1,), (1,)), ((), ()))

    def scores(hh):
        cols = slice(hh * 2 * PEER_HALF, (hh + 1) * 2 * PEER_HALF)
        q = jnp.dot(h2_ref[...], wq_ref[:, cols], preferred_element_type=F32).astype(BF16)
        return (lax.dot_general(key_ref[hh, 0], q[:, :PEER_HALF], nt, preferred_element_type=F32),
                lax.dot_general(key_ref[hh, 1], q[:, PEER_HALF:], nt, preferred_element_type=F32))

    nxt = scores(0)
    for hh in range(n_heads):
        s1, s2 = nxt
        if hh + 1 < n_heads:
            nxt = scores(hh + 1)
        _route_head(s1, s2, cnt_ref.at[hh], e1_ref.at[hh], rk2_ref.at[hh], e2_ref.at[hh])


def _route_head(s1, s2, cnt_ref, e1_ref, rk2_ref, e2_ref):
    t = s1.shape[1]
    n_tiles = PEER_KEYS // F32_ROWS
    tiles = lambda s: [s[r * F32_ROWS:(r + 1) * F32_ROWS] for r in range(n_tiles)]
    sub = lax.broadcasted_iota(jnp.int32, (F32_ROWS, t), 0).astype(F32)
    neg = -jnp.inf

    s1t, s2t = tiles(s1), tiles(s2)
    work1, work2 = list(s1t), list(s2t)
    rank2 = [jnp.full((F32_ROWS, t), float(PEER_TOPK), F32)] * n_tiles
    a, b, knock1 = [], [], []
    for k in range(PEER_TOPK):
        m, kn = _extract_round(work1, sub, k)
        a.append(m)
        knock1.append(kn)
        m, _ = _extract_round(work2, sub, k, rank2)
        b.append(m)

    def spread(vs):
        out = vs[0]
        for i in range(1, F32_ROWS):
            out = jnp.where(sub == float(i), vs[i], out)
        return out

    b_lo, b_hi, a_hi = spread(b[:F32_ROWS]), spread(b[F32_ROWS:]), spread(a[F32_ROWS:])
    b_r2, b_r4 = pltpu.roll(b_lo, 2, 0), pltpu.roll(b_lo, 4, 0)
    cand = [
        a[0] + b_lo,
        a[0] + b_hi,
        a[1] + b_lo,
        jnp.where(sub < 5.0, a[2] + b_lo, neg),
        jnp.where(sub < 4.0, a[3] + b_lo, jnp.where(sub < 7.0, a[4] + b_r4, neg)),
        jnp.where(sub < 2.0, a[5] + b_lo,
                  jnp.where(sub < 4.0, a[6] + b_r2, jnp.where(sub < 6.0, a[7] + b_r4, neg))),
        a_hi + b[0],
    ]
    work = list(cand)
    for k in range(PEER_TOPK):
        _extract_round(work, sub, k)
    sel = [jnp.where(w < c, 1.0, 0.0) for w, c in zip(work, cand)]
    v0 = a[0] + b[0]
    z = _allsub(_tree([s * jnp.exp(c - v0) for s, c in zip(sel, cand)], jnp.add), jnp.add)

    count = lambda x: _allsub(x, jnp.add)
    c = [count(sel[0] + sel[1]), count(sel[2]), count(sel[3]),
         count(jnp.where(sub < 4.0, sel[4], 0.0)), count(jnp.where(sub < 4.0, 0.0, sel[4])),
         count(jnp.where(sub < 2.0, sel[5], 0.0)),
         count(jnp.where(sub < 2.0, 0.0, jnp.where(sub < 4.0, sel[5], 0.0))),
         count(jnp.where(sub < 4.0, 0.0, sel[5]))]
    c += [count(jnp.where(sub == float(i), sel[6], 0.0)) for i in range(F32_ROWS)]

    inv_z = 1.0 / z
    for r in range(n_tiles):
        rows = slice(r * F32_ROWS, (r + 1) * F32_ROWS)
        cnt = jnp.zeros((F32_ROWS, t), F32)
        for p in range(PEER_TOPK):
            cnt = jnp.where(knock1[p] == float(r), c[p], cnt)
        e1 = jnp.where(work1[r] == neg, jnp.exp(s1t[r] - a[0]), 0.0)
        for lt in range(t // LANES):
            cnt_ref[lt, rows, :] = cnt[:, lt * LANES:(lt + 1) * LANES]
            e1_ref[lt, rows, :] = e1[:, lt * LANES:(lt + 1) * LANES]
    for r in range(0, n_tiles, 2):
        rows = slice(r * F32_ROWS, (r + 2) * F32_ROWS)
        rk = jnp.concatenate([rank2[r], rank2[r + 1]], axis=0)
        ex = jnp.concatenate([jnp.exp(s2t[r] - b[0]), jnp.exp(s2t[r + 1] - b[0])], axis=0)
        inv = jnp.concatenate([inv_z, inv_z], axis=0)
        rk2_ref[rows, :] = rk.astype(BF16)
        e2_ref[rows, :] = jnp.where(rk < PEER_TOPK, ex * inv, 0.0).astype(BF16)


def _route(h2, wq, keys, tf):
    n, d = h2.shape
    hp = ROUTE_HEADS_PER_STEP
    by_i = pl.BlockSpec((hp, tf // LANES, PEER_KEYS, LANES), lambda i, h: (h, i, 0, 0))
    by_j = pl.BlockSpec((hp, PEER_KEYS, tf), lambda i, h: (h, 0, i))
    by_i_shape = jax.ShapeDtypeStruct((PEER_HEADS, n // LANES, PEER_KEYS, LANES), F32)
    by_j_shape = jax.ShapeDtypeStruct((PEER_HEADS, PEER_KEYS, n), BF16)
    return pl.pallas_call(
        _route_kernel,
        grid=(n // tf, PEER_HEADS // hp),
        in_specs=[pl.BlockSpec((tf, d), lambda i, h: (i, 0)),
                  pl.BlockSpec((d, hp * 2 * PEER_HALF), lambda i, h: (0, h)),
                  pl.BlockSpec((hp, 2, PEER_KEYS, PEER_HALF), lambda i, h: (h, 0, 0, 0))],
        out_specs=[by_i, by_i, by_j, by_j],
        out_shape=[by_i_shape, by_i_shape, by_j_shape, by_j_shape],
        compiler_params=_params(2),
        name="route",
    )(h2, wq, keys)


def _row_on_all_sublanes(ref, lead, row):
    return ref[(*lead, pl.ds(row, BF16_ROWS, stride=0), slice(None))]


def _peer_kernel(h2_ref, x1_ref, u_ref, vt_ref, cnt_ref, e1_ref, rk2_ref, e2_ref, gfin_ref,
                 o_ref, acc_ref, a_ref, w_ref, h2s_ref, *, n_j, normalise):
    s = pl.program_id(0)
    sb = jnp.maximum(s - 1, 0)
    cur = s % 2
    prev = 1 - cur
    ne, t = a_ref.shape
    d = acc_ref.shape[0]
    n_i = ne // PEER_KEYS

    @pl.when(s == 0)
    def _():
        w_ref[...] = jnp.zeros_like(w_ref)

    @pl.when(sb % n_j == 0)
    def _():
        acc_ref[...] = jnp.zeros_like(acc_ref)

    @pl.when(s % n_j == 0)
    def _():
        h2s_ref[...] = h2_ref[...]

    def mm1(m):
        rows = slice(m * MM_ROWS, (m + 1) * MM_ROWS)
        for n in range(t // MXU_DIM):
            cols = slice(n * MXU_DIM, (n + 1) * MXU_DIM)
            a_ref[rows, cols] = lax.dot_general(u_ref[rows, :], h2s_ref[cols, :],
                                                (((1,), (1,)), ((), ())),
                                                preferred_element_type=F32)

    def mm2(m, n):
        rows = slice(m * MM_ROWS, (m + 1) * MM_ROWS)
        cols = slice(n * MXU_DIM, (n + 1) * MXU_DIM)
        acc_ref[rows, cols] += jnp.dot(vt_ref[rows, :], w_ref[prev, :, cols],
                                       preferred_element_type=F32)

    def gate_block(ii):
        rows = slice(ii * PEER_KEYS, (ii + 1) * PEER_KEYS)
        per = MXU_DIM // LANES
        for n in range(t // MXU_DIM):
            cols = slice(n * MXU_DIM, (n + 1) * MXU_DIM)
            g = None
            for h in range(PEER_HEADS):
                cnt = jnp.concatenate(
                    [_row_on_all_sublanes(cnt_ref, (h, n * per + k), ii) for k in range(per)],
                    axis=1)
                e1 = jnp.concatenate(
                    [_row_on_all_sublanes(e1_ref, (h, n * per + k), ii) for k in range(per)],
                    axis=1)
                cnt = pltpu.repeat(cnt.astype(BF16), PEER_KEYS // BF16_ROWS, 0)
                e1 = pltpu.repeat(e1.astype(BF16), PEER_KEYS // BF16_ROWS, 0)
                term = jnp.where(rk2_ref[h, :, cols] < cnt, e2_ref[h, :, cols] * e1,
                                 jnp.zeros((), BF16))
                g = term if g is None else g + term
            w_ref[cur, rows, cols] = g * _gelu_tanh(a_ref[rows, cols].astype(BF16))

    n_cols = t // MXU_DIM
    ii_per_m = MM_ROWS // PEER_KEYS
    pieces2 = [(m, n) for m in range(d // MM_ROWS) for n in range(n_cols)]
    mm1(0)
    for ii in range(n_i):
        if ii % ii_per_m == 0 and ii // ii_per_m + 1 < ne // MM_ROWS:
            mm1(ii // ii_per_m + 1)
        if (ii * len(pieces2)) % n_i == 0:
            mm2(*pieces2[ii * len(pieces2) // n_i])
        gate_block(ii)

    @pl.when(jnp.logical_and(s >= 1, sb % n_j == n_j - 1))
    def _():
        x2 = x1_ref[...] + acc_ref[...].T
        o_ref[...] = _rms(x2, gfin_ref[...]) if normalise else x2


def _peer(h2, x1, u, vt, cnt, e1, rk2, e2, g_final, tp, ne, normalise):
    n, d = h2.shape
    n_exp = u.shape[0]
    n_i = ne // PEER_KEYS
    n_j = n_exp // ne
    n_steps = (n // tp) * n_j
    assert ne % MM_ROWS == 0 and d % MM_ROWS == 0 and tp % MXU_DIM == 0
    assert (n_i * MXU_DIM * MM_ROWS) % (d * tp) == 0
    sa = lambda s: jnp.minimum(s, n_steps - 1)
    sb = lambda s: jnp.maximum(s - 1, 0)
    tok_a = pl.BlockSpec((tp, d), lambda s: (sa(s) // n_j, 0))
    tok_b = pl.BlockSpec((tp, d), lambda s: (sb(s) // n_j, 0))
    by_i = pl.BlockSpec((PEER_HEADS, tp // LANES, n_i, LANES),
                        lambda s: (0, sa(s) // n_j, sa(s) % n_j, 0))
    by_j = pl.BlockSpec((PEER_HEADS, PEER_KEYS, tp), lambda s: (0, 0, sa(s) // n_j))
    return pl.pallas_call(
        functools.partial(_peer_kernel, n_j=n_j, normalise=normalise),
        grid=(n_steps + 1,),
        in_specs=[tok_a, tok_b,
                  pl.BlockSpec((ne, d), lambda s: (sa(s) % n_j, 0)),
                  pl.BlockSpec((d, ne), lambda s: (0, sb(s) % n_j)),
                  by_i, by_i, by_j, by_j,
                  pl.BlockSpec(g_final.shape, lambda s: (0, 0))],
        out_specs=tok_b,
        out_shape=jax.ShapeDtypeStruct((n, d), F32),
        scratch_shapes=[pltpu.VMEM((d, tp), F32), pltpu.VMEM((ne, tp), F32),
                        pltpu.VMEM((2, ne, tp), BF16), pltpu.VMEM((tp, d), BF16)],
        compiler_params=_params(1),
        name="peer",
    )(h2, x1, u, vt, cnt, e1, rk2, e2, g_final)


def _tiles(batch, seq):
    n = batch * seq
    tm = min(256, seq)
    tf = min(512, n)
    tp = min(512, n)
    return tm, tf, tp


def kernel(x, mem, positions, g_mix, g_mem, w_in, gate_b, attn_sink, conv_w, conv_b, lru_wr, lru_br,
           lru_wi, lru_bi, lru_lambda, w_mem_kv, p_attn, p_lru, p_mem, w_out, g_ffn, w_peer_q,
           peer_sub_keys, peer_u, peer_v, g_final):
    batch, seq, d = x.shape
    mem_len = mem.shape[1]
    depth = w_in.shape[0]
    n = batch * seq
    n_exp = peer_u.shape[1]
    tm, tf, tp = _tiles(batch, seq)
    ne = min(2048, n_exp)

    inv_freq = ROPE_THETA ** (-jnp.arange(0, ROPE_DIM, 2, dtype=F32) / ROPE_DIM)
    invf = jnp.zeros((1, HEAD_DIM), F32).at[0, :ROPE_DIM].set(jnp.tile(inv_freq, 2))
    pos2d = positions.reshape(n, 1)
    offs = [int(o) for o in np.cumsum([0, ATTN_Q_W + 2 * ATTN_KV_W, 2 * LRU_WIDTH, MEM_Q_W])]
    mem2d = mem.reshape(batch * mem_len, d)

    xc = x.reshape(n, d)
    out = None
    for l in range(depth):
        w = w_in[l].astype(BF16)
        wqkv, wlru, wqm, wg = (w[:, offs[0]:offs[1]], w[:, offs[1]:offs[2]], w[:, offs[2]:offs[3]],
                               w[:, offs[3]:])
        q, k, v, xl, gg, qm, gs = _inproj(xc, pos2d, g_mix[l][None], invf, wqkv, wlru, wqm, wg,
                                          gate_b[l][None], tm)
        attn = _attn(attn_sink[l], q, k, v, batch, seq)
        lru = _lru(xl, gg, conv_w[l], conv_b[l][None], lru_wr[l].astype(BF16), lru_br[l],
                   lru_wi[l].astype(BF16), lru_bi[l], lru_lambda[l], batch, seq)
        km, vm = _memkv(mem2d, g_mem[l][None], w_mem_kv[l].astype(BF16), batch, mem_len)
        x1, h2 = _merge(xc, attn, lru, qm, gs, km, vm, p_attn[l].astype(BF16),
                        p_lru[l].astype(BF16), p_mem[l].astype(BF16), w_out[l].astype(BF16),
                        g_ffn[l][None], tm, seq, mem_len)
        cnt, e1, rk2, e2 = _route(h2, w_peer_q[l].astype(BF16), peer_sub_keys[l].astype(BF16), tf)
        last = l == depth - 1
        out = _peer(h2, x1, peer_u[l].astype(BF16), peer_v[l].astype(BF16).T, cnt, e1, rk2, e2,
                    g_final[None], tp, ne, normalise=last)
        xc = out
    return out.reshape(batch, seq, d)
```
